```python
import math
import jax, jax.numpy as jnp
from jax import lax
import numpy as np

D_MODEL = 1024
BATCH = 16
SEQ = 2048
DEPTH = 4
DEC_BATCH = 4
DEC_SEQ = 4096
PAST_LEN = 128

N_HEADS = 4
HEAD_DIM = 64
V_DIM = 2 * HEAD_DIM
QK_W = N_HEADS * 2 * HEAD_DIM
ATTN_W = N_HEADS * V_DIM
N_FGROUPS = 4
FGROUP_DIM = 128
FOURIER_W = N_FGROUPS * FGROUP_DIM
IN_W = 2 * QK_W + ATTN_W + FOURIER_W
D_FF = 2816
ROPE_THETA = 10000.0
Q_BLOCK = 128
EPS = 1e-6
LAMBDA_STD = 0.1

kernel_name = "hybrid_diffattn_fnet_macaron_encoder"


def rmsnorm(x, g):
    xf = x.astype(jnp.float32)
    y = xf * lax.rsqrt(jnp.mean(xf * xf, axis=-1, keepdims=True) + EPS)
    return (y * g.astype(jnp.float32)).astype(x.dtype)


def swiglu(h, w_up, w_down):
    gate, up = jnp.split(h @ w_up, 2, axis=-1)
    return (jax.nn.silu(gate) * up) @ w_down


def rope_tables(seq, dtype):
    inv = 1.0 / (ROPE_THETA ** (jnp.arange(0, HEAD_DIM, 2, dtype=jnp.float32) / HEAD_DIM))
    ang = jnp.arange(seq, dtype=jnp.float32)[:, None] * inv[None, :]
    ang = jnp.concatenate([ang, ang], axis=-1)
    return jnp.cos(ang).astype(dtype), jnp.sin(ang).astype(dtype)


def apply_rope(t, cos, sin):
    t1, t2 = jnp.split(t, 2, axis=-1)
    rot = jnp.concatenate([-t2, t1], axis=-1)
    c = cos[None, :, None, None, :]
    s = sin[None, :, None, None, :]
    return t * c + rot * s


def diff_attention(q, k, v, lam):
    B, S = q.shape[0], q.shape[1]
    nb = S // Q_BLOCK
    qb = q.reshape(B, nb, Q_BLOCK, N_HEADS, 2, HEAD_DIM).transpose(1, 0, 2, 3, 4, 5)
    scale = HEAD_DIM ** -0.5

    def block(q_blk):
        s = jnp.einsum('bqhcd,bkhcd->bhcqk', q_blk, k).astype(jnp.float32) * scale
        p = jax.nn.softmax(s, axis=-1)
        a = p[:, :, 0] - lam * p[:, :, 1]
        return jnp.einsum('bhqk,bkhe->bqhe', a.astype(v.dtype), v)

    o = lax.map(block, qb)
    return o.transpose(1, 0, 2, 3, 4).reshape(B, S, N_HEADS, V_DIM)


def fourier_mix(u):
    B, S = u.shape[0], u.shape[1]
    ug = u.reshape(B, S, N_FGROUPS, FGROUP_DIM).astype(jnp.float32)
    f = jnp.fft.fftn(ug, axes=(1, 3), norm="ortho").real
    return f.reshape(B, S, FOURIER_W).astype(u.dtype)


def trunk(x, g_ff1, w_ff1_up, w_ff1_down, g_mix, w_in, lam_q1, lam_k1, lam_q2, lam_k2,
          g_sub, w_pa, w_pf, w_gate, w_o, g_ff2, w_ff2_up, w_ff2_down, g_final):
    B, S, _ = x.shape
    cos, sin = rope_tables(S, x.dtype)
    for l in range(DEPTH):
        x = x + 0.5 * swiglu(rmsnorm(x, g_ff1[l]), w_ff1_up[l], w_ff1_down[l])
        h = rmsnorm(x, g_mix[l])
        proj = h @ w_in[l]
        q, k, v, u = jnp.split(proj, [QK_W, 2 * QK_W, 2 * QK_W + ATTN_W], axis=-1)
        q = apply_rope(q.reshape(B, S, N_HEADS, 2, HEAD_DIM), cos, sin)
        k = apply_rope(k.reshape(B, S, N_HEADS, 2, HEAD_DIM), cos, sin)
        v = v.reshape(B, S, N_HEADS, V_DIM)
        lam_init = 0.8 - 0.6 * math.exp(-0.3 * l)
        lam = (jnp.exp(jnp.sum(lam_q1[l].astype(jnp.float32) * lam_k1[l].astype(jnp.float32)))
               - jnp.exp(jnp.sum(lam_q2[l].astype(jnp.float32) * lam_k2[l].astype(jnp.float32)))
               + lam_init)
        o = diff_attention(q, k, v, lam)
        o = rmsnorm(o, g_sub[l]) * (1.0 - lam_init)
        br_a = o.reshape(B, S, ATTN_W) @ w_pa[l]
        br_f = fourier_mix(u) @ w_pf[l]
        g_a, g_f = jnp.split(jax.nn.sigmoid(h @ w_gate[l]), 2, axis=-1)
        x = x + (g_a * br_a + g_f * br_f) @ w_o[l]
        x = x + 0.5 * swiglu(rmsnorm(x, g_ff2[l]), w_ff2_up[l], w_ff2_down[l])
    return rmsnorm(x, g_final)


def setup_inputs(seed: int = 0) -> dict:
    key = jax.random.key(seed)
    ks = jax.random.split(key, 24)
    f32 = jnp.float32

    def w(k, shape, fan_in):
        return jax.random.normal(k, shape, f32) * (fan_in ** -0.5)

    def gain(k, shape):
        return 1.0 + 0.02 * jax.random.normal(k, shape, f32)

    return {
        "x_prompt": jax.random.normal(ks[0], (BATCH, SEQ, D_MODEL), f32),
        "x_sample": jax.random.normal(ks[1], (DEC_BATCH, DEC_SEQ, D_MODEL), f32),
        "g_ff1": gain(ks[2], (DEPTH, D_MODEL)),
        "w_ff1_up": w(ks[3], (DEPTH, D_MODEL, 2 * D_FF), D_MODEL),
        "w_ff1_down": w(ks[4], (DEPTH, D_FF, D_MODEL), D_FF),
        "g_mix": gain(ks[5], (DEPTH, D_MODEL)),
        "w_in": w(ks[6], (DEPTH, D_MODEL, IN_W), D_MODEL),
        "lam_q1": LAMBDA_STD * jax.random.normal(ks[7], (DEPTH, HEAD_DIM), f32),
        "lam_k1": LAMBDA_STD * jax.random.normal(ks[8], (DEPTH, HEAD_DIM), f32),
        "lam_q2": LAMBDA_STD * jax.random.normal(ks[9], (DEPTH, HEAD_DIM), f32),
        "lam_k2": LAMBDA_STD * jax.random.normal(ks[10], (DEPTH, HEAD_DIM), f32),
        "g_sub": gain(ks[11], (DEPTH, V_DIM)),
        "w_pa": w(ks[12], (DEPTH, ATTN_W, D_MODEL), ATTN_W),
        "w_pf": w(ks[13], (DEPTH, FOURIER_W, D_MODEL), FOURIER_W),
        "w_gate": w(ks[14], (DEPTH, D_MODEL, 2 * D_MODEL), D_MODEL),
        "w_o": w(ks[15], (DEPTH, D_MODEL, D_MODEL), D_MODEL),
        "g_ff2": gain(ks[16], (DEPTH, D_MODEL)),
        "w_ff2_up": w(ks[17], (DEPTH, D_MODEL, 2 * D_FF), D_MODEL),
        "w_ff2_down": w(ks[18], (DEPTH, D_FF, D_MODEL), D_FF),
        "g_final": gain(ks[19], (D_MODEL,)),
    }


def reference(x_prompt, x_sample, g_ff1, w_ff1_up, w_ff1_down, g_mix, w_in, lam_q1, lam_k1,
              lam_q2, lam_k2, g_sub, w_pa, w_pf, w_gate, w_o, g_ff2, w_ff2_up, w_ff2_down, g_final):
    y_prompt = trunk(x_prompt, g_ff1, w_ff1_up, w_ff1_down, g_mix, w_in, lam_q1, lam_k1, lam_q2,
                     lam_k2, g_sub, w_pa, w_pf, w_gate, w_o, g_ff2, w_ff2_up, w_ff2_down, g_final)
    y_sample = trunk(x_sample, g_ff1, w_ff1_up, w_ff1_down, g_mix, w_in, lam_q1, lam_k1, lam_q2,
                     lam_k2, g_sub, w_pa, w_pf, w_gate, w_o, g_ff2, w_ff2_up, w_ff2_down, g_final)
    return (y_prompt, y_sample)
```

```python
import functools
import math

import jax
import jax.numpy as jnp
from jax import lax
from jax.experimental import pallas as pl
from jax.experimental.pallas import tpu as pltpu

D_MODEL = 1024
DEPTH = 4
N_HEADS = 4
HEAD_DIM = 64
V_DIM = 2 * HEAD_DIM
QK_W = N_HEADS * 2 * HEAD_DIM
ATTN_W = N_HEADS * V_DIM
N_FGROUPS = 4
FGROUP_DIM = 128
FOURIER_W = N_FGROUPS * FGROUP_DIM
IN_W = 2 * QK_W + ATTN_W + FOURIER_W
D_FF = 2816
ROPE_THETA = 10000.0
EPS = 1e-6

LANES = 128
QKV_W = 2 * QK_W + ATTN_W
VMEM_LIMIT = 56 * 1024 * 1024

F32 = jnp.float32
BF16 = jnp.bfloat16


def _rms(x, g):
    return x * lax.rsqrt(jnp.mean(x * x, axis=-1, keepdims=True) + EPS) * g


def _dot(a, b):
    return jnp.dot(a, b, preferred_element_type=F32)


def _params(sem):
    return pltpu.CompilerParams(dimension_semantics=sem, vmem_limit_bytes=VMEM_LIMIT)


def _const_spec(shape):
    return pl.BlockSpec(shape, lambda *_: (0,) * len(shape), pipeline_mode=pl.Buffered(1))


def _ffn_kernel(x_ref, g_ref, wg_ref, wu_ref, wd_ref, gf_ref, o_ref, *, final_norm):
    x = x_ref[...]
    h = _rms(x, g_ref[...]).astype(BF16)
    gate = _dot(h, wg_ref[...])
    up = _dot(h, wu_ref[...])
    act = (gate / (1.0 + jnp.exp(-gate)) * up).astype(BF16)
    y = x + 0.5 * _dot(act, wd_ref[...])
    if final_norm:
        y = _rms(y, gf_ref[...])
    o_ref[...] = y


def _ffn(x, g, w_up, w_down, g_final, *, final_norm, tm):
    t = x.shape[0]
    return pl.pallas_call(
        functools.partial(_ffn_kernel, final_norm=final_norm),
        grid=(t // tm,),
        in_specs=[
            pl.BlockSpec((tm, D_MODEL), lambda i: (i, 0)),
            _const_spec((1, D_MODEL)),
            pl.BlockSpec((D_MODEL, D_FF), lambda i: (0, 0), pipeline_mode=pl.Buffered(1)),
            pl.BlockSpec((D_MODEL, D_FF), lambda i: (0, 1), pipeline_mode=pl.Buffered(1)),
            _const_spec((D_FF, D_MODEL)),
            _const_spec((1, D_MODEL)),
        ],
        out_specs=pl.BlockSpec((tm, D_MODEL), lambda i: (i, 0)),
        out_shape=jax.ShapeDtypeStruct((t, D_MODEL), F32),
        compiler_params=_params(("parallel",)),
        name="ffn",
    )(x, g, w_up, w_up, w_down, g_final)


def _pre_kernel(x_ref, g_ref, win_ref, wgate_ref, cos_ref, sin_ref, dft_ref,
                qkv_ref, fab_ref, gate_ref):
    h = _rms(x_ref[...], g_ref[...]).astype(BF16)
    proj = _dot(h, win_ref[...])
    cos = cos_ref[...]
    sin = sin_ref[...]
    lane = lax.broadcasted_iota(jnp.int32, cos.shape, 1)
    low_half = (lane % HEAD_DIM) < (HEAD_DIM // 2)
    for j in range(2 * QK_W // LANES):
        t = proj[:, j * LANES:(j + 1) * LANES]
        rot = jnp.where(low_half,
                        pltpu.roll(t, LANES - HEAD_DIM // 2, axis=1),
                        pltpu.roll(t, HEAD_DIM // 2, axis=1))
        roped = t * cos + rot * sin
        if j < QK_W // LANES:
            roped = roped * (HEAD_DIM ** -0.5)
        qkv_ref[:, j * LANES:(j + 1) * LANES] = roped.astype(BF16)
    qkv_ref[:, 2 * QK_W:] = proj[:, 2 * QK_W:QKV_W].astype(BF16)
    dft = dft_ref[...]
    for grp in range(N_FGROUPS):
        u = proj[:, QKV_W + grp * FGROUP_DIM:QKV_W + (grp + 1) * FGROUP_DIM].astype(BF16)
        ab = _dot(u, dft)
        fab_ref[:, grp * FGROUP_DIM:(grp + 1) * FGROUP_DIM] = ab[:, :FGROUP_DIM].astype(BF16)
        fab_ref[:, FOURIER_W + grp * FGROUP_DIM:FOURIER_W + (grp + 1) * FGROUP_DIM] = (
            ab[:, FGROUP_DIM:].astype(BF16))
    z = _dot(h, wgate_ref[...])
    gate_ref[...] = (1.0 / (1.0 + jnp.exp(-z))).astype(BF16)


def _pre(x, g, w_in, w_gate, cos, sin, dft, *, n_prompt, s_prompt, s_sample, tm):
    t = x.shape[0]
    n_prompt_tiles = n_prompt // tm

    def pos_map(i):
        p = jnp.where(i < n_prompt_tiles, i % (s_prompt // tm), (i - n_prompt_tiles) % (s_sample // tm))
        return (p, 0)

    return pl.pallas_call(
        _pre_kernel,
        grid=(t // tm,),
        in_specs=[
            pl.BlockSpec((tm, D_MODEL), lambda i: (i, 0)),
            _const_spec((1, D_MODEL)),
            _const_spec((D_MODEL, IN_W)),
            _const_spec((D_MODEL, 2 * D_MODEL)),
            pl.BlockSpec((tm, LANES), pos_map),
            pl.BlockSpec((tm, LANES), pos_map),
            _const_spec((FGROUP_DIM, 2 * FGROUP_DIM)),
        ],
        out_specs=[
            pl.BlockSpec((tm, QKV_W), lambda i: (i, 0)),
            pl.BlockSpec((tm, 2 * FOURIER_W), lambda i: (i, 0)),
            pl.BlockSpec((tm, 2 * D_MODEL), lambda i: (i, 0)),
        ],
        out_shape=[
            jax.ShapeDtypeStruct((t, QKV_W), BF16),
            jax.ShapeDtypeStruct((t, 2 * FOURIER_W), BF16),
            jax.ShapeDtypeStruct((t, 2 * D_MODEL), BF16),
        ],
        compiler_params=_params(("parallel",)),
        name="mixer_in",
    )(x, g, w_in, w_gate, cos, sin, dft)


def _attn_kernel(lamv_ref, q_ref, k_ref, v_ref, gsub_ref, *rest, lam_init):
    o_ref = rest[-1]
    q = q_ref[...]
    k = k_ref[...]
    lane = lax.broadcasted_iota(jnp.int32, q.shape, 1)
    zero = jnp.zeros_like(q)
    contract_last = (((1,), (1,)), ((), ()))
    s0 = lax.dot_general(jnp.where(lane < HEAD_DIM, q, zero), k, contract_last, preferred_element_type=F32)
    s1 = lax.dot_general(jnp.where(lane >= HEAD_DIM, q, zero), k, contract_last, preferred_element_type=F32)
    p0 = jnp.exp(s0 - jnp.max(s0, axis=-1, keepdims=True))
    p1 = jnp.exp(s1 - jnp.max(s1, axis=-1, keepdims=True))
    l0 = jnp.sum(p0, axis=-1, keepdims=True)
    l1 = jnp.sum(p1, axis=-1, keepdims=True)
    lamv = lamv_ref[...]
    lam = (jnp.exp(jnp.sum(lamv[0:1] * lamv[1:2], axis=-1, keepdims=True))
           - jnp.exp(jnp.sum(lamv[2:3] * lamv[3:4], axis=-1, keepdims=True)) + lam_init)
    a = p0 * (1.0 / l0) - p1 * (lam / l1)
    o = _dot(a.astype(BF16), v_ref[...])
    o_ref[...] = (_rms(o, gsub_ref[...]) * (1.0 - lam_init)).astype(BF16)


def _attn(qkv, lamv, gsub, o_prev, *, row0, batch, seq, tq, lam_init):
    t = qkv.shape[0]
    b0 = row0 // seq
    q0 = row0 // tq
    nq = seq // tq
    in_specs = [
        _const_spec((4, HEAD_DIM)),
        pl.BlockSpec((tq, LANES), lambda b, h, i: (q0 + b * nq + i, h)),
        pl.BlockSpec((seq, LANES), lambda b, h, i: (b0 + b, N_HEADS + h)),
        pl.BlockSpec((seq, LANES), lambda b, h, i: (b0 + b, 2 * N_HEADS + h)),
        _const_spec((1, V_DIM)),
    ]
    args = [lamv, qkv, qkv, qkv, gsub]
    aliases = {}
    if o_prev is not None:
        in_specs.append(pl.BlockSpec(memory_space=pl.ANY))
        args.append(o_prev)
        aliases = {len(args) - 1: 0}
    return pl.pallas_call(
        functools.partial(_attn_kernel, lam_init=lam_init),
        grid=(batch, N_HEADS, nq),
        in_specs=in_specs,
        out_specs=pl.BlockSpec((tq, LANES), lambda b, h, i: (q0 + b * nq + i, h)),
        out_shape=jax.ShapeDtypeStruct((t, ATTN_W), BF16),
        input_output_aliases=aliases,
        compiler_params=_params(("parallel", "parallel", "parallel")),
        name=f"diff_attn_s{seq}",
    )(*args)


def _seqdft_kernel(c_ref, s_ref, fab_ref, *rest, scale):
    o_ref = rest[-1]
    out = _dot(c_ref[...], fab_ref[:, :FOURIER_W]) - _dot(s_ref[...], fab_ref[:, FOURIER_W:])
    o_ref[...] = (out * scale).astype(BF16)


def _seqdft(fab, ctab, stab, f_prev, *, row0, batch, seq, tq):
    t = fab.shape[0]
    b0 = row0 // seq
    q0 = row0 // tq
    nq = seq // tq
    in_specs = [
        pl.BlockSpec((tq, seq), lambda b, i: (i, 0)),
        pl.BlockSpec((tq, seq), lambda b, i: (i, 0)),
        pl.BlockSpec((seq, 2 * FOURIER_W), lambda b, i: (b0 + b, 0)),
    ]
    args = [ctab, stab, fab]
    aliases = {}
    if f_prev is not None:
        in_specs.append(pl.BlockSpec(memory_space=pl.ANY))
        args.append(f_prev)
        aliases = {len(args) - 1: 0}
    return pl.pallas_call(
        functools.partial(_seqdft_kernel, scale=1.0 / math.sqrt(seq * FGROUP_DIM)),
        grid=(batch, nq),
        in_specs=in_specs,
        out_specs=pl.BlockSpec((tq, FOURIER_W), lambda b, i: (q0 + b * nq + i, 0)),
        out_shape=jax.ShapeDtypeStruct((t, FOURIER_W), BF16),
        input_output_aliases=aliases,
        compiler_params=_params(("parallel", "parallel")),
        name=f"seq_dft_s{seq}",
    )(*args)


def _post_kernel(x_ref, o_ref, f_ref, gate_ref, wpa_ref, wpf_ref, wo_ref, out_ref):
    br_a = _dot(o_ref[...], wpa_ref[...])
    br_f = _dot(f_ref[...], wpf_ref[...])
    merged = (gate_ref[:, :D_MODEL].astype(F32) * br_a + gate_ref[:, D_MODEL:].astype(F32) * br_f)
    out_ref[...] = x_ref[...] + _dot(merged.astype(BF16), wo_ref[...])


def _post(x, o, f, gate, w_pa, w_pf, w_o, *, tm):
    t = x.shape[0]
    return pl.pallas_call(
        _post_kernel,
        grid=(t // tm,),
        in_specs=[
            pl.BlockSpec((tm, D_MODEL), lambda i: (i, 0)),
            pl.BlockSpec((tm, ATTN_W), lambda i: (i, 0)),
            pl.BlockSpec((tm, FOURIER_W), lambda i: (i, 0)),
            pl.BlockSpec((tm, 2 * D_MODEL), lambda i: (i, 0)),
            _const_spec((ATTN_W, D_MODEL)),
            _const_spec((FOURIER_W, D_MODEL)),
            _const_spec((D_MODEL, D_MODEL)),
        ],
        out_specs=pl.BlockSpec((tm, D_MODEL), lambda i: (i, 0)),
        out_shape=jax.ShapeDtypeStruct((t, D_MODEL), F32),
        compiler_params=_params(("parallel",)),
        name="mixer_out",
    )(x, o, f, gate, w_pa, w_pf, w_o)


def _rope_tables(seq):
    inv = 1.0 / (ROPE_THETA ** (jnp.arange(0, HEAD_DIM, 2, dtype=F32) / HEAD_DIM))
    ang = jnp.arange(seq, dtype=F32)[:, None] * inv[None, :]
    ang = jnp.concatenate([ang, ang, ang, ang], axis=-1)
    lane = jnp.arange(LANES)
    sign = jnp.where((lane % HEAD_DIM) < HEAD_DIM // 2, -1.0, 1.0).astype(F32)
    return jnp.cos(ang), jnp.sin(ang) * sign[None, :]


def _dft_tables(n):
    j = jnp.arange(n, dtype=jnp.int32)
    ang = ((j[:, None] * j[None, :]) % n).astype(F32) * (2.0 * math.pi / n)
    return jnp.cos(ang), jnp.sin(ang)


def kernel(x_prompt, x_sample, g_ff1, w_ff1_up, w_ff1_down, g_mix, w_in, lam_q1, lam_k1, lam_q2, lam_k2,
           g_sub, w_pa, w_pf, w_gate, w_o, g_ff2, w_ff2_up, w_ff2_down, g_final):
    bp, sp, _ = x_prompt.shape
    bs, ss, _ = x_sample.shape
    n_prompt = bp * sp
    assert n_prompt % ss == 0 and sp <= ss
    x = jnp.concatenate([x_prompt.reshape(n_prompt, D_MODEL), x_sample.reshape(bs * ss, D_MODEL)], axis=0)

    cos, sin = _rope_tables(ss)
    cg, sg = _dft_tables(FGROUP_DIM)
    dft = jnp.concatenate([cg, sg], axis=1).astype(BF16)
    seq_tabs = {s: tuple(tab.astype(BF16) for tab in _dft_tables(s)) for s in {sp, ss}}

    row = lambda v: v.reshape(1, -1)
    gf = row(g_final)
    for l in range(DEPTH):
        lam_init = 0.8 - 0.6 * math.exp(-0.3 * l)
        x = _ffn(x, row(g_ff1[l]), w_ff1_up[l].astype(BF16), w_ff1_down[l].astype(BF16), gf,
                 final_norm=False, tm=256)
        qkv, fab, gate = _pre(x, row(g_mix[l]), w_in[l].astype(BF16), w_gate[l].astype(BF16),
                              cos, sin, dft, n_prompt=n_prompt, s_prompt=sp, s_sample=ss, tm=256)
        lamv = jnp.stack([lam_q1[l], lam_k1[l], lam_q2[l], lam_k2[l]])
        o = _attn(qkv, lamv, row(g_sub[l]), None, row0=0, batch=bp, seq=sp, tq=256, lam_init=lam_init)
        o = _attn(qkv, lamv, row(g_sub[l]), o, row0=n_prompt, batch=bs, seq=ss, tq=256, lam_init=lam_init)
        f = _seqdft(fab, *seq_tabs[sp], None, row0=0, batch=bp, seq=sp, tq=256)
        f = _seqdft(fab, *seq_tabs[ss], f, row0=n_prompt, batch=bs, seq=ss, tq=256)
        x = _post(x, o, f, gate, w_pa[l].astype(BF16), w_pf[l].astype(BF16), w_o[l].astype(BF16), tm=256)
        x = _ffn(x, row(g_ff2[l]), w_ff2_up[l].astype(BF16), w_ff2_down[l].astype(BF16), gf,
                 final_norm=(l == DEPTH - 1), tm=256)
    return x[:n_prompt].reshape(bp, sp, D_MODEL), x[n_prompt:].reshape(bs, ss, D_MODEL)
```

```python
import functools
import math

import jax
import jax.numpy as jnp
from jax import lax
from jax.experimental import pallas as pl
from jax.experimental.pallas import tpu as pltpu

D_MODEL = 1024
DEPTH = 4
N_HEADS = 4
HEAD_DIM = 64
V_DIM = 2 * HEAD_DIM
QK_W = N_HEADS * 2 * HEAD_DIM
ATTN_W = N_HEADS * V_DIM
N_FGROUPS = 4
FGROUP_DIM = 128
FOURIER_W = N_FGROUPS * FGROUP_DIM
IN_W = 2 * QK_W + ATTN_W + FOURIER_W
D_FF = 2816
ROPE_THETA = 10000.0
EPS = 1e-6

LANES = 128
QKV_W = 2 * QK_W + ATTN_W
Q_SCALE = HEAD_DIM ** -0.5 * math.log2(math.e)
VMEM_LIMIT = 56 * 1024 * 1024
ATTN_TQ = 256
ATTN_KC = 512

F32 = jnp.float32
BF16 = jnp.bfloat16


def _rms(x, g):
    return x * lax.rsqrt(jnp.mean(x * x, axis=-1, keepdims=True) + EPS) * g


def _dot(a, b):
    return jnp.dot(a, b, preferred_element_type=F32)


def _params(sem):
    return pltpu.CompilerParams(dimension_semantics=sem, vmem_limit_bytes=VMEM_LIMIT)


def _const_spec(shape):
    return pl.BlockSpec(shape, lambda *_: (0,) * len(shape), pipeline_mode=pl.Buffered(1))


def _ffn_kernel(x_ref, g_ref, wg_ref, wu_ref, wd_ref, gf_ref, o_ref, *, final_norm):
    x = x_ref[...]
    h = _rms(x, g_ref[...]).astype(BF16)
    gate = _dot(h, wg_ref[...])
    up = _dot(h, wu_ref[...])
    act = (gate / (1.0 + jnp.exp(-gate)) * up).astype(BF16)
    y = x + 0.5 * _dot(act, wd_ref[...])
    if final_norm:
        y = _rms(y, gf_ref[...])
    o_ref[...] = y


def _ffn(x, g, w_up, w_down, g_final, *, final_norm, tm):
    t = x.shape[0]
    return pl.pallas_call(
        functools.partial(_ffn_kernel, final_norm=final_norm),
        grid=(t // tm,),
        in_specs=[
            pl.BlockSpec((tm, D_MODEL), lambda i: (i, 0)),
            _const_spec((1, D_MODEL)),
            pl.BlockSpec((D_MODEL, D_FF), lambda i: (0, 0), pipeline_mode=pl.Buffered(1)),
            pl.BlockSpec((D_MODEL, D_FF), lambda i: (0, 1), pipeline_mode=pl.Buffered(1)),
            _const_spec((D_FF, D_MODEL)),
            _const_spec((1, D_MODEL)),
        ],
        out_specs=pl.BlockSpec((tm, D_MODEL), lambda i: (i, 0)),
        out_shape=jax.ShapeDtypeStruct((t, D_MODEL), F32),
        compiler_params=_params(("parallel",)),
        name="ffn",
    )(x, g, w_up, w_up, w_down, g_final)


def _pre_kernel(x_ref, g_ref, win_ref, wgate_ref, cos_ref, sin_ref, dft_ref,
                q_ref, kt_ref, v_ref, fab_ref, gate_ref):
    h = _rms(x_ref[...], g_ref[...]).astype(BF16)
    proj = _dot(h, win_ref[...])
    cos = cos_ref[...]
    sin = sin_ref[...]
    lane = lax.broadcasted_iota(jnp.int32, cos.shape, 1)
    low_half = (lane % HEAD_DIM) < (HEAD_DIM // 2)
    for j in range(2 * N_HEADS):
        t = proj[:, j * LANES:(j + 1) * LANES]
        rot = jnp.where(low_half,
                        pltpu.roll(t, LANES - HEAD_DIM // 2, axis=1),
                        pltpu.roll(t, HEAD_DIM // 2, axis=1))
        roped = t * cos + rot * sin
        if j < N_HEADS:
            q_ref[:, j * LANES:(j + 1) * LANES] = (roped * Q_SCALE).astype(BF16)
        else:
            kt_ref[(j - N_HEADS) * LANES:(j - N_HEADS + 1) * LANES, :] = roped.T.astype(BF16)
    v_ref[...] = proj[:, 2 * QK_W:QKV_W].astype(BF16)
    dft = dft_ref[...]
    for grp in range(N_FGROUPS):
        u = proj[:, QKV_W + grp * FGROUP_DIM:QKV_W + (grp + 1) * FGROUP_DIM].astype(BF16)
        ab = _dot(u, dft)
        fab_ref[:, grp * FGROUP_DIM:(grp + 1) * FGROUP_DIM] = ab[:, :FGROUP_DIM].astype(BF16)
        fab_ref[:, FOURIER_W + grp * FGROUP_DIM:FOURIER_W + (grp + 1) * FGROUP_DIM] = (
            ab[:, FGROUP_DIM:].astype(BF16))
    z = _dot(h, wgate_ref[...])
    gate_ref[...] = (1.0 / (1.0 + jnp.exp(-z))).astype(BF16)


def _pre(x, g, w_in, w_gate, cos, sin, dft, *, n_prompt, s_prompt, s_sample, tm):
    t = x.shape[0]
    n_prompt_tiles = n_prompt // tm

    def pos_map(i):
        p = jnp.where(i < n_prompt_tiles, i % (s_prompt // tm), (i - n_prompt_tiles) % (s_sample // tm))
        return (p, 0)

    return pl.pallas_call(
        _pre_kernel,
        grid=(t // tm,),
        in_specs=[
            pl.BlockSpec((tm, D_MODEL), lambda i: (i, 0)),
            _const_spec((1, D_MODEL)),
            _const_spec((D_MODEL, IN_W)),
            _const_spec((D_MODEL, 2 * D_MODEL)),
            pl.BlockSpec((tm, LANES), pos_map),
            pl.BlockSpec((tm, LANES), pos_map),
            _const_spec((FGROUP_DIM, 2 * FGROUP_DIM)),
        ],
        out_specs=[
            pl.BlockSpec((tm, QK_W), lambda i: (i, 0)),
            pl.BlockSpec((QK_W, tm), lambda i: (0, i)),
            pl.BlockSpec((tm, ATTN_W), lambda i: (i, 0)),
            pl.BlockSpec((tm, 2 * FOURIER_W), lambda i: (i, 0)),
            pl.BlockSpec((tm, 2 * D_MODEL), lambda i: (i, 0)),
        ],
        out_shape=[
            jax.ShapeDtypeStruct((t, QK_W), BF16),
            jax.ShapeDtypeStruct((QK_W, t), BF16),
            jax.ShapeDtypeStruct((t, ATTN_W), BF16),
            jax.ShapeDtypeStruct((t, 2 * FOURIER_W), BF16),
            jax.ShapeDtypeStruct((t, 2 * D_MODEL), BF16),
        ],
        compiler_params=_params(("parallel",)),
        name="mixer_in",
    )(x, g, w_in, w_gate, cos, sin, dft)


def _attn_kernel(lamv_ref, q_ref, kt_ref, v_ref, gsub_ref, *rest, lam_init, kc):
    o_ref = rest[-7]
    s_scr = (rest[-6:-4], rest[-4:-2])
    p_scr = rest[-2:]
    tq, seq = q_ref.shape[0], kt_ref.shape[1]
    nblk = kc // LANES
    lamv = lamv_ref[...]
    lam = (jnp.exp(jnp.sum(lamv[0:1] * lamv[1:2], axis=-1, keepdims=True))
           - jnp.exp(jnp.sum(lamv[2:3] * lamv[3:4], axis=-1, keepdims=True)) + lam_init)
    gsub = gsub_ref[...]
    lane = lax.broadcasted_iota(jnp.int32, (tq, LANES), 1)
    row_max, row_sum = {}, {}

    def blocks(c):
        return [slice(c * kc + j * LANES, c * kc + (j + 1) * LANES) for j in range(nblk)]

    def tree(op, xs):
        while len(xs) > 1:
            xs = [op(xs[i], xs[i + 1]) for i in range(0, len(xs) - 1, 2)] + ([xs[-1]] if len(xs) % 2 else [])
        return xs[0]

    def scores(h):
        cols = slice(h * LANES, (h + 1) * LANES)
        q = q_ref[:, cols]
        zero = jnp.zeros_like(q)
        qs = (jnp.where(lane < HEAD_DIM, q, zero), jnp.where(lane >= HEAD_DIM, q, zero))
        acc = [None, None]
        for c in range(seq // kc):
            kt = kt_ref[cols, c * kc:(c + 1) * kc]
            for comp in range(2):
                s = _dot(qs[comp], kt)
                s_scr[h % 2][comp][:, c * kc:(c + 1) * kc] = s
                part = tree(jnp.maximum, [s[:, j * LANES:(j + 1) * LANES] for j in range(nblk)])
                acc[comp] = part if acc[comp] is None else jnp.maximum(acc[comp], part)
            yield
        row_max[h] = [jnp.broadcast_to(jnp.max(a, axis=-1, keepdims=True), (tq, LANES)) for a in acc]

    def probs(h):
        acc = [None, None]
        for c in range(seq // kc):
            for comp in range(2):
                ps = []
                for blk in blocks(c):
                    ps.append(jnp.exp2(s_scr[h % 2][comp][:, blk] - row_max[h][comp]))
                    p_scr[comp][:, blk] = ps[-1].astype(BF16)
                part = tree(jnp.add, ps)
                acc[comp] = part if acc[comp] is None else acc[comp] + part
            yield
        row_sum[h] = [jnp.sum(a, axis=-1, keepdims=True) for a in acc]

    def values(h):
        cols = slice(h * LANES, (h + 1) * LANES)
        l0, l1 = row_sum[h]
        coef = jnp.broadcast_to(lam * l0 / l1, (tq, LANES)).astype(BF16)
        acc = None
        for c in range(seq // kc):
            a = jnp.concatenate([p_scr[0][:, blk] - coef * p_scr[1][:, blk] for blk in blocks(c)], axis=1)
            d = _dot(a, v_ref[c * kc:(c + 1) * kc, cols])
            acc = d if acc is None else acc + d
        o = acc * (1.0 / l0)
        o_ref[:, cols] = (_rms(o, gsub) * (1.0 - lam_init)).astype(BF16)

    for _ in scores(0):
        pass
    for h in range(N_HEADS):
        nxt = scores(h + 1) if h + 1 < N_HEADS else iter(())
        for _ in probs(h):
            next(nxt, None)
        for _ in nxt:
            pass
        values(h)


def _attn(q, kt, v, lamv, gsub, o_prev, *, row0, batch, seq, tq, lam_init):
    t = q.shape[0]
    b0 = row0 // seq
    q0 = row0 // tq
    nq = seq // tq
    in_specs = [
        _const_spec((4, HEAD_DIM)),
        pl.BlockSpec((tq, QK_W), lambda b, i: (q0 + b * nq + i, 0)),
        pl.BlockSpec((QK_W, seq), lambda b, i: (0, b0 + b)),
        pl.BlockSpec((seq, ATTN_W), lambda b, i: (b0 + b, 0)),
        _const_spec((1, V_DIM)),
    ]
    args = [lamv, q, kt, v, gsub]
    aliases = {}
    if o_prev is not None:
        in_specs.append(pl.BlockSpec(memory_space=pl.ANY))
        args.append(o_prev)
        aliases = {len(args) - 1: 0}
    return pl.pallas_call(
        functools.partial(_attn_kernel, lam_init=lam_init, kc=ATTN_KC),
        grid=(batch, nq),
        in_specs=in_specs,
        out_specs=pl.BlockSpec((tq, ATTN_W), lambda b, i: (q0 + b * nq + i, 0)),
        out_shape=jax.ShapeDtypeStruct((t, ATTN_W), BF16),
        scratch_shapes=[pltpu.VMEM((tq, seq), F32)] * 4 + [pltpu.VMEM((tq, seq), BF16)] * 2,
        input_output_aliases=aliases,
        compiler_params=_params(("parallel", "parallel")),
        name=f"diff_attn_s{seq}",
    )(*args)


def _seqdft_kernel(c_ref, s_ref, fab_ref, *rest, scale):
    o_ref = rest[-1]
    out = _dot(c_ref[...], fab_ref[:, :FOURIER_W]) - _dot(s_ref[...], fab_ref[:, FOURIER_W:])
    o_ref[...] = (out * scale).astype(BF16)


def _seqdft(fab, ctab, stab, f_prev, *, row0, batch, seq, tq):
    t = fab.shape[0]
    b0 = row0 // seq
    q0 = row0 // tq
    nq = seq // tq
    in_specs = [
        pl.BlockSpec((tq, seq), lambda b, i: (i, 0)),
        pl.BlockSpec((tq, seq), lambda b, i: (i, 0)),
        pl.BlockSpec((seq, 2 * FOURIER_W), lambda b, i: (b0 + b, 0)),
    ]
    args = [ctab, stab, fab]
    aliases = {}
    if f_prev is not None:
        in_specs.append(pl.BlockSpec(memory_space=pl.ANY))
        args.append(f_prev)
        aliases = {len(args) - 1: 0}
    return pl.pallas_call(
        functools.partial(_seqdft_kernel, scale=1.0 / math.sqrt(seq * FGROUP_DIM)),
        grid=(batch, nq),
        in_specs=in_specs,
        out_specs=pl.BlockSpec((tq, FOURIER_W), lambda b, i: (q0 + b * nq + i, 0)),
        out_shape=jax.ShapeDtypeStruct((t, FOURIER_W), BF16),
        input_output_aliases=aliases,
        compiler_params=_params(("parallel", "parallel")),
        name=f"seq_dft_s{seq}",
    )(*args)


def _post_kernel(x_ref, o_ref, f_ref, gate_ref, wpa_ref, wpf_ref, wo_ref, out_ref):
    br_a = _dot(o_ref[...], wpa_ref[...])
    br_f = _dot(f_ref[...], wpf_ref[...])
    merged = (gate_ref[:, :D_MODEL].astype(F32) * br_a + gate_ref[:, D_MODEL:].astype(F32) * br_f)
    out_ref[...] = x_ref[...] + _dot(merged.astype(BF16), wo_ref[...])


def _post(x, o, f, gate, w_pa, w_pf, w_o, *, tm):
    t = x.shape[0]
    return pl.pallas_call(
        _post_kernel,
        grid=(t // tm,),
        in_specs=[
            pl.BlockSpec((tm, D_MODEL), lambda i: (i, 0)),
            pl.BlockSpec((tm, ATTN_W), lambda i: (i, 0)),
            pl.BlockSpec((tm, FOURIER_W), lambda i: (i, 0)),
            pl.BlockSpec((tm, 2 * D_MODEL), lambda i: (i, 0)),
            _const_spec((ATTN_W, D_MODEL)),
            _const_spec((FOURIER_W, D_MODEL)),
            _const_spec((D_MODEL, D_MODEL)),
        ],
        out_specs=pl.BlockSpec((tm, D_MODEL), lambda i: (i, 0)),
        out_shape=jax.ShapeDtypeStruct((t, D_MODEL), F32),
        compiler_params=_params(("parallel",)),
        name="mixer_out",
    )(x, o, f, gate, w_pa, w_pf, w_o)


def _rope_tables(seq):
    inv = 1.0 / (ROPE_THETA ** (jnp.arange(0, HEAD_DIM, 2, dtype=F32) / HEAD_DIM))
    ang = jnp.arange(seq, dtype=F32)[:, None] * inv[None, :]
    ang = jnp.concatenate([ang, ang, ang, ang], axis=-1)
    lane = jnp.arange(LANES)
    sign = jnp.where((lane % HEAD_DIM) < HEAD_DIM // 2, -1.0, 1.0).astype(F32)
    return jnp.cos(ang), jnp.sin(ang) * sign[None, :]


def _dft_tables(n):
    j = jnp.arange(n, dtype=jnp.int32)
    ang = ((j[:, None] * j[None, :]) % n).astype(F32) * (2.0 * math.pi / n)
    return jnp.cos(ang), jnp.sin(ang)


def kernel(x_prompt, x_sample, g_ff1, w_ff1_up, w_ff1_down, g_mix, w_in, lam_q1, lam_k1, lam_q2, lam_k2,
           g_sub, w_pa, w_pf, w_gate, w_o, g_ff2, w_ff2_up, w_ff2_down, g_final):
    bp, sp, _ = x_prompt.shape
    bs, ss, _ = x_sample.shape
    n_prompt = bp * sp
    assert n_prompt % ss == 0 and sp <= ss
    x = jnp.concatenate([x_prompt.reshape(n_prompt, D_MODEL), x_sample.reshape(bs * ss, D_MODEL)], axis=0)

    cos, sin = _rope_tables(ss)
    cg, sg = _dft_tables(FGROUP_DIM)
    dft = jnp.concatenate([cg, sg], axis=1).astype(BF16)
    seq_tabs = {s: tuple(tab.astype(BF16) for tab in _dft_tables(s)) for s in {sp, ss}}

    row = lambda v: v.reshape(1, -1)
    gf = row(g_final)
    for l in range(DEPTH):
        lam_init = 0.8 - 0.6 * math.exp(-0.3 * l)
        x = _ffn(x, row(g_ff1[l]), w_ff1_up[l].astype(BF16), w_ff1_down[l].astype(BF16), gf,
                 final_norm=False, tm=256)
        q, kt, v, fab, gate = _pre(x, row(g_mix[l]), w_in[l].astype(BF16), w_gate[l].astype(BF16),
                                   cos, sin, dft, n_prompt=n_prompt, s_prompt=sp, s_sample=ss, tm=256)
        lamv = jnp.stack([lam_q1[l], lam_k1[l], lam_q2[l], lam_k2[l]])
        o = _attn(q, kt, v, lamv, row(g_sub[l]), None, row0=0, batch=bp, seq=sp, tq=ATTN_TQ,
                  lam_init=lam_init)
        o = _attn(q, kt, v, lamv, row(g_sub[l]), o, row0=n_prompt, batch=bs, seq=ss, tq=ATTN_TQ,
                  lam_init=lam_init)
        f = _seqdft(fab, *seq_tabs[sp], None, row0=0, batch=bp, seq=sp, tq=256)
        f = _seqdft(fab, *seq_tabs[ss], f, row0=n_prompt, batch=bs, seq=ss, tq=256)
        x = _post(x, o, f, gate, w_pa[l].astype(BF16), w_pf[l].astype(BF16), w_o[l].astype(BF16), tm=256)
        x = _ffn(x, row(g_ff2[l]), w_ff2_up[l].astype(BF16), w_ff2_down[l].astype(BF16), gf,
                 final_norm=(l == DEPTH - 1), tm=256)
    return x[:n_prompt].reshape(bp, sp, D_MODEL), x[n_prompt:].reshape(bs, ss, D_MODEL)
```

```python
import functools
import math

import jax
import jax.numpy as jnp
from jax import lax
from jax.experimental import pallas as pl
from jax.experimental.pallas import tpu as pltpu

D_MODEL = 1024
DEPTH = 4
N_HEADS = 4
HEAD_DIM = 64
V_DIM = 2 * HEAD_DIM
QK_W = N_HEADS * 2 * HEAD_DIM
ATTN_W = N_HEADS * V_DIM
N_FGROUPS = 4
FGROUP_DIM = 128
FOURIER_W = N_FGROUPS * FGROUP_DIM
IN_W = 2 * QK_W + ATTN_W + FOURIER_W
D_FF = 2816
ROPE_THETA = 10000.0
EPS = 1e-6

LANES = 128
QKV_W = 2 * QK_W + ATTN_W
Q_SCALE = HEAD_DIM ** -0.5 * math.log2(math.e)
VMEM_LIMIT = 56 * 1024 * 1024
TOKEN_TILE = 512
ATTN_TQ = 256
DFT_TQ = 256
DFT_ROWS = 8192
DFT_SPLIT = 64

F32 = jnp.float32
BF16 = jnp.bfloat16


def _rms(x, g):
    return x * lax.rsqrt(jnp.mean(x * x, axis=-1, keepdims=True) + EPS) * g


def _dot(a, b):
    return jnp.dot(a, b, preferred_element_type=F32)


def _params(sem):
    return pltpu.CompilerParams(dimension_semantics=sem, vmem_limit_bytes=VMEM_LIMIT)


def _const_spec(shape):
    return pl.BlockSpec(shape, lambda *_: (0,) * len(shape), pipeline_mode=pl.Buffered(1))


def _ffn_kernel(x_ref, g_ref, wg_ref, wu_ref, wd_ref, gf_ref, *rest, final_norm):
    o_ref = rest[-1]
    x = x_ref[...]
    h = _rms(x, g_ref[...]).astype(BF16)
    gate = _dot(h, wg_ref[...])
    up = _dot(h, wu_ref[...])
    act = (gate / (1.0 + jnp.exp(-gate)) * up).astype(BF16)
    y = x + 0.5 * _dot(act, wd_ref[...])
    if final_norm:
        y = _rms(y, gf_ref[...])
    o_ref[...] = y


def _ffn(x, g, w_up, w_down, g_final, *, final_norm, rows=None, in_row0=0, out_rows=None, out_row0=0,
         out_prev=None):
    tm = TOKEN_TILE
    rows = x.shape[0] if rows is None else rows
    out_rows = rows if out_rows is None else out_rows
    in0, out0 = in_row0 // tm, out_row0 // tm
    in_specs = [
        pl.BlockSpec((tm, D_MODEL), lambda i: (in0 + i, 0)),
        _const_spec((1, D_MODEL)),
        pl.BlockSpec((D_MODEL, D_FF), lambda i: (0, 0), pipeline_mode=pl.Buffered(1)),
        pl.BlockSpec((D_MODEL, D_FF), lambda i: (0, 1), pipeline_mode=pl.Buffered(1)),
        _const_spec((D_FF, D_MODEL)),
        _const_spec((1, D_MODEL)),
    ]
    args = [x, g, w_up, w_up, w_down, g_final]
    aliases = {}
    if out_prev is not None:
        in_specs.append(pl.BlockSpec(memory_space=pl.ANY))
        args.append(out_prev)
        aliases = {len(args) - 1: 0}
    return pl.pallas_call(
        functools.partial(_ffn_kernel, final_norm=final_norm),
        grid=(rows // tm,),
        in_specs=in_specs,
        out_specs=pl.BlockSpec((tm, D_MODEL), lambda i: (out0 + i, 0)),
        out_shape=jax.ShapeDtypeStruct((out_rows, D_MODEL), F32),
        input_output_aliases=aliases,
        compiler_params=_params(("parallel",)),
        name="ffn",
    )(*args)


def _pre_kernel(x_ref, g_ref, win_ref, wgate_ref, cos_ref, sin_ref, dft_ref,
                q_ref, kt_ref, v_ref, fab_ref, gate_ref):
    h = _rms(x_ref[...], g_ref[...]).astype(BF16)
    proj = _dot(h, win_ref[...])
    cos = cos_ref[...]
    sin = sin_ref[...]
    lane = lax.broadcasted_iota(jnp.int32, cos.shape, 1)
    low_half = (lane % HEAD_DIM) < (HEAD_DIM // 2)
    for j in range(2 * N_HEADS):
        t = proj[:, j * LANES:(j + 1) * LANES]
        rot = jnp.where(low_half,
                        pltpu.roll(t, LANES - HEAD_DIM // 2, axis=1),
                        pltpu.roll(t, HEAD_DIM // 2, axis=1))
        roped = t * cos + rot * sin
        if j < N_HEADS:
            q_ref[:, j * LANES:(j + 1) * LANES] = (roped * Q_SCALE).astype(BF16)
        else:
            kt_ref[(j - N_HEADS) * LANES:(j - N_HEADS + 1) * LANES, :] = roped.T.astype(BF16)
    v_ref[...] = proj[:, 2 * QK_W:QKV_W].astype(BF16)
    dft = dft_ref[...]
    for grp in range(N_FGROUPS):
        u = proj[:, QKV_W + grp * FGROUP_DIM:QKV_W + (grp + 1) * FGROUP_DIM].astype(BF16)
        ab = _dot(u, dft)
        fab_ref[:, grp * FGROUP_DIM:(grp + 1) * FGROUP_DIM] = ab[:, :FGROUP_DIM].astype(BF16)
        fab_ref[:, FOURIER_W + grp * FGROUP_DIM:FOURIER_W + (grp + 1) * FGROUP_DIM] = (
            ab[:, FGROUP_DIM:].astype(BF16))
    z = _dot(h, wgate_ref[...])
    gate_ref[...] = (1.0 / (1.0 + jnp.exp(-z))).astype(BF16)


def _pre(x, g, w_in, w_gate, cos, sin, dft, *, n_prompt, s_prompt, s_sample):
    t = x.shape[0]
    tm = TOKEN_TILE
    n_prompt_tiles = n_prompt // tm

    def pos_map(i):
        p = jnp.where(i < n_prompt_tiles, i % (s_prompt // tm), (i - n_prompt_tiles) % (s_sample // tm))
        return (p, 0)

    return pl.pallas_call(
        _pre_kernel,
        grid=(t // tm,),
        in_specs=[
            pl.BlockSpec((tm, D_MODEL), lambda i: (i, 0)),
            _const_spec((1, D_MODEL)),
            _const_spec((D_MODEL, IN_W)),
            _const_spec((D_MODEL, 2 * D_MODEL)),
            pl.BlockSpec((tm, LANES), pos_map),
            pl.BlockSpec((tm, LANES), pos_map),
            _const_spec((FGROUP_DIM, 2 * FGROUP_DIM)),
        ],
        out_specs=[
            pl.BlockSpec((tm, QK_W), lambda i: (i, 0)),
            pl.BlockSpec((QK_W, tm), lambda i: (0, i)),
            pl.BlockSpec((tm, ATTN_W), lambda i: (i, 0)),
            pl.BlockSpec((tm, 2 * FOURIER_W), lambda i: (i, 0)),
            pl.BlockSpec((tm, 2 * D_MODEL), lambda i: (i, 0)),
        ],
        out_shape=[
            jax.ShapeDtypeStruct((t, QK_W), BF16),
            jax.ShapeDtypeStruct((QK_W, t), BF16),
            jax.ShapeDtypeStruct((t, ATTN_W), BF16),
            jax.ShapeDtypeStruct((t, 2 * FOURIER_W), BF16),
            jax.ShapeDtypeStruct((t, 2 * D_MODEL), BF16),
        ],
        compiler_params=_params(("parallel",)),
        name="mixer_in",
    )(x, g, w_in, w_gate, cos, sin, dft)


def _attn_kernel(lamv_ref, q_ref, kt_ref, v_ref, gsub_ref, *rest, lam_init):
    o_ref = rest[-1]
    q = q_ref[...]
    kt = kt_ref[...]
    lane = lax.broadcasted_iota(jnp.int32, q.shape, 1)
    zero = jnp.zeros_like(q)
    s0 = _dot(jnp.where(lane < HEAD_DIM, q, zero), kt)
    s1 = _dot(jnp.where(lane >= HEAD_DIM, q, zero), kt)
    p0 = jnp.exp2(s0 - jnp.max(s0, axis=-1, keepdims=True))
    p1 = jnp.exp2(s1 - jnp.max(s1, axis=-1, keepdims=True))
    l0 = jnp.sum(p0, axis=-1, keepdims=True)
    l1 = jnp.sum(p1, axis=-1, keepdims=True)
    lamv = lamv_ref[...]
    lam = (jnp.exp(jnp.sum(lamv[0:1] * lamv[1:2], axis=-1, keepdims=True))
           - jnp.exp(jnp.sum(lamv[2:3] * lamv[3:4], axis=-1, keepdims=True)) + lam_init)
    a = (p0 - (lam * l0 / l1) * p1).astype(BF16)
    o = _dot(a, v_ref[...]) * (1.0 / l0)
    o_ref[...] = (_rms(o, gsub_ref[...]) * (1.0 - lam_init)).astype(BF16)


def _attn(q, kt, v, lamv, gsub, o_prev, *, row0, batch, seq, lam_init):
    t = q.shape[0]
    tq = ATTN_TQ
    b0 = row0 // seq
    q0 = row0 // tq
    nq = seq // tq
    in_specs = [
        _const_spec((4, HEAD_DIM)),
        pl.BlockSpec((tq, LANES), lambda b, h, i: (q0 + b * nq + i, h)),
        pl.BlockSpec((LANES, seq), lambda b, h, i: (h, b0 + b)),
        pl.BlockSpec((seq, V_DIM), lambda b, h, i: (b0 + b, h)),
        _const_spec((1, V_DIM)),
    ]
    args = [lamv, q, kt, v, gsub]
    aliases = {}
    if o_prev is not None:
        in_specs.append(pl.BlockSpec(memory_space=pl.ANY))
        args.append(o_prev)
        aliases = {len(args) - 1: 0}
    return pl.pallas_call(
        functools.partial(_attn_kernel, lam_init=lam_init),
        grid=(batch, N_HEADS, nq),
        in_specs=in_specs,
        out_specs=pl.BlockSpec((tq, V_DIM), lambda b, h, i: (q0 + b * nq + i, h)),
        out_shape=jax.ShapeDtypeStruct((t, ATTN_W), BF16),
        input_output_aliases=aliases,
        compiler_params=_params(("parallel", "parallel", "parallel")),
        name=f"diff_attn_s{seq}",
    )(*args)


def _seqdft_kernel(c_ref, s_ref, fab_ref, *rest, scale, seq):
    o_ref = rest[-1]
    r0 = pl.multiple_of(pl.program_id(2) * seq, seq)
    out = (_dot(c_ref[...], fab_ref[pl.ds(r0, seq), :FOURIER_W])
           - _dot(s_ref[...], fab_ref[pl.ds(r0, seq), FOURIER_W:]))
    o_ref[...] = (out * scale).astype(BF16)


def _seqdft(fab, ctab, stab, f_prev, *, row0, batch, seq):
    t = fab.shape[0]
    tq = DFT_TQ
    nb = min(batch, DFT_ROWS // seq)
    g0 = row0 // (nb * seq)
    q0 = row0 // tq
    nq = seq // tq
    in_specs = [
        pl.BlockSpec((tq, seq), lambda g, i, j: (i, 0)),
        pl.BlockSpec((tq, seq), lambda g, i, j: (i, 0)),
        pl.BlockSpec((nb * seq, 2 * FOURIER_W), lambda g, i, j: (g0 + g, 0)),
    ]
    args = [ctab, stab, fab]
    aliases = {}
    if f_prev is not None:
        in_specs.append(pl.BlockSpec(memory_space=pl.ANY))
        args.append(f_prev)
        aliases = {len(args) - 1: 0}
    return pl.pallas_call(
        functools.partial(_seqdft_kernel, scale=1.0 / math.sqrt(seq * FGROUP_DIM), seq=seq),
        grid=(batch // nb, nq, nb),
        in_specs=in_specs,
        out_specs=pl.BlockSpec((tq, FOURIER_W), lambda g, i, j: (q0 + (g * nb + j) * nq + i, 0)),
        out_shape=jax.ShapeDtypeStruct((t, FOURIER_W), BF16),
        input_output_aliases=aliases,
        compiler_params=_params(("parallel", "parallel", "arbitrary")),
        name=f"seq_dft_s{seq}",
    )(*args)


def _post_kernel(x_ref, o_ref, f_ref, gate_ref, wpa_ref, wpf_ref, wo_ref, out_ref):
    br_a = _dot(o_ref[...], wpa_ref[...])
    br_f = _dot(f_ref[...], wpf_ref[...])
    merged = (gate_ref[:, :D_MODEL].astype(F32) * br_a + gate_ref[:, D_MODEL:].astype(F32) * br_f)
    out_ref[...] = x_ref[...] + _dot(merged.astype(BF16), wo_ref[...])


def _post(x, o, f, gate, w_pa, w_pf, w_o):
    t = x.shape[0]
    tm = TOKEN_TILE
    return pl.pallas_call(
        _post_kernel,
        grid=(t // tm,),
        in_specs=[
            pl.BlockSpec((tm, D_MODEL), lambda i: (i, 0)),
            pl.BlockSpec((tm, ATTN_W), lambda i: (i, 0)),
            pl.BlockSpec((tm, FOURIER_W), lambda i: (i, 0)),
            pl.BlockSpec((tm, 2 * D_MODEL), lambda i: (i, 0)),
            _const_spec((ATTN_W, D_MODEL)),
            _const_spec((FOURIER_W, D_MODEL)),
            _const_spec((D_MODEL, D_MODEL)),
        ],
        out_specs=pl.BlockSpec((tm, D_MODEL), lambda i: (i, 0)),
        out_shape=jax.ShapeDtypeStruct((t, D_MODEL), F32),
        compiler_params=_params(("parallel",)),
        name="mixer_out",
    )(x, o, f, gate, w_pa, w_pf, w_o)


def _rope_tables(seq):
    inv = 1.0 / (ROPE_THETA ** (jnp.arange(0, HEAD_DIM, 2, dtype=F32) / HEAD_DIM))
    ang = jnp.arange(seq, dtype=F32)[:, None] * inv[None, :]
    ang = jnp.concatenate([ang, ang, ang, ang], axis=-1)
    lane = jnp.arange(LANES)
    sign = jnp.where((lane % HEAD_DIM) < HEAD_DIM // 2, -1.0, 1.0).astype(F32)
    return jnp.cos(ang), jnp.sin(ang) * sign[None, :]


def _dft_tables(rows, cols, n):
    j = jnp.arange(rows, dtype=jnp.int32)
    k = jnp.arange(cols, dtype=jnp.int32)
    ang = ((j[:, None] * k[None, :]) % n).astype(F32) * (2.0 * math.pi / n)
    return jnp.cos(ang), jnp.sin(ang)


def _seq_tables_kernel(ca_ref, sa_ref, cb_ref, sb_ref, c_ref, s_ref):
    cb, sb = cb_ref[...], sb_ref[...]
    na = c_ref.shape[0] // DFT_SPLIT
    for a in range(na):
        row = pl.ds(pl.program_id(0) * na + a, 1)
        ca, sa = ca_ref[row, :], sa_ref[row, :]
        c_ref[a * DFT_SPLIT:(a + 1) * DFT_SPLIT, :] = (ca * cb - sa * sb).astype(BF16)
        s_ref[a * DFT_SPLIT:(a + 1) * DFT_SPLIT, :] = (sa * cb + ca * sb).astype(BF16)


def _seq_tables(n):
    ca, sa = _dft_tables(n // DFT_SPLIT, n, n // DFT_SPLIT)
    cb, sb = _dft_tables(DFT_SPLIT, n, n)
    rows = DFT_TQ
    return pl.pallas_call(
        _seq_tables_kernel,
        grid=(n // rows,),
        in_specs=[_const_spec((n // DFT_SPLIT, n))] * 2 + [_const_spec((DFT_SPLIT, n))] * 2,
        out_specs=[pl.BlockSpec((rows, n), lambda i: (i, 0))] * 2,
        out_shape=[jax.ShapeDtypeStruct((n, n), BF16)] * 2,
        compiler_params=_params(("parallel",)),
        name=f"dft_tables_{n}",
    )(ca, sa, cb, sb)


def kernel(x_prompt, x_sample, g_ff1, w_ff1_up, w_ff1_down, g_mix, w_in, lam_q1, lam_k1, lam_q2, lam_k2,
           g_sub, w_pa, w_pf, w_gate, w_o, g_ff2, w_ff2_up, w_ff2_down, g_final):
    bp, sp, _ = x_prompt.shape
    bs, ss, _ = x_sample.shape
    n_prompt = bp * sp
    assert n_prompt % ss == 0 and sp <= ss
    n_sample = bs * ss
    n_tok = n_prompt + n_sample

    cos, sin = _rope_tables(ss)
    cg, sg = _dft_tables(FGROUP_DIM, FGROUP_DIM, FGROUP_DIM)
    dft = jnp.concatenate([cg, sg], axis=1).astype(BF16)
    seq_tabs = {s: _seq_tables(s) for s in {sp, ss}}

    row = lambda v: v.reshape(1, -1)
    gf = row(g_final)
    x = None
    for l in range(DEPTH):
        lam_init = 0.8 - 0.6 * math.exp(-0.3 * l)
        ff1 = (row(g_ff1[l]), w_ff1_up[l].astype(BF16), w_ff1_down[l].astype(BF16), gf)
        if l == 0:
            x = _ffn(x_prompt.reshape(n_prompt, D_MODEL), *ff1, final_norm=False, out_rows=n_tok)
            x = _ffn(x_sample.reshape(n_sample, D_MODEL), *ff1, final_norm=False, out_rows=n_tok,
                     out_row0=n_prompt, out_prev=x)
        else:
            x = _ffn(x, *ff1, final_norm=False)
        q, kt, v, fab, gate = _pre(x, row(g_mix[l]), w_in[l].astype(BF16), w_gate[l].astype(BF16),
                                   cos, sin, dft, n_prompt=n_prompt, s_prompt=sp, s_sample=ss)
        lamv = jnp.stack([lam_q1[l], lam_k1[l], lam_q2[l], lam_k2[l]])
        o = _attn(q, kt, v, lamv, row(g_sub[l]), None, row0=0, batch=bp, seq=sp, lam_init=lam_init)
        o = _attn(q, kt, v, lamv, row(g_sub[l]), o, row0=n_prompt, batch=bs, seq=ss, lam_init=lam_init)
        f = _seqdft(fab, *seq_tabs[sp], None, row0=0, batch=bp, seq=sp)
        f = _seqdft(fab, *seq_tabs[ss], f, row0=n_prompt, batch=bs, seq=ss)
        x = _post(x, o, f, gate, w_pa[l].astype(BF16), w_pf[l].astype(BF16), w_o[l].astype(BF16))
        ff2 = (row(g_ff2[l]), w_ff2_up[l].astype(BF16), w_ff2_down[l].astype(BF16), gf)
        if l < DEPTH - 1:
            x = _ffn(x, *ff2, final_norm=False)
    y_prompt = _ffn(x, *ff2, final_norm=True, rows=n_prompt)
    y_sample = _ffn(x, *ff2, final_norm=True, rows=n_sample, in_row0=n_prompt)
    return y_prompt.reshape(bp, sp, D_MODEL), y_sample.reshape(bs, ss, D_MODEL)
```

```python
import functools
import math

import jax
import jax.numpy as jnp
from jax import lax
from jax.experimental import pallas as pl
from jax.experimental.pallas import tpu as pltpu

D_MODEL = 1024
DEPTH = 4
N_HEADS = 4
HEAD_DIM = 64
V_DIM = 2 * HEAD_DIM
QK_W = N_HEADS * 2 * HEAD_DIM
ATTN_W = N_HEADS * V_DIM
N_FGROUPS = 4
FGROUP_DIM = 128
FOURIER_W = N_FGROUPS * FGROUP_DIM
IN_W = 2 * QK_W + ATTN_W + FOURIER_W
D_FF = 2816
ROPE_THETA = 10000.0
EPS = 1e-6

LANES = 128
QKV_W = 2 * QK_W + ATTN_W
Q_SCALE = HEAD_DIM ** -0.5 * math.log2(math.e)
VMEM_LIMIT = 56 * 1024 * 1024
TOKEN_TILE = 512
ATTN_TQ = 256
DFT_TQ = 256
DFT_ROWS = 8192
DFT_SPLIT = 64

F32 = jnp.float32
BF16 = jnp.bfloat16


def _rms(x, g):
    return x * lax.rsqrt(jnp.mean(x * x, axis=-1, keepdims=True) + EPS) * g


def _dot(a, b):
    return jnp.dot(a, b, preferred_element_type=F32)


def _params(sem):
    return pltpu.CompilerParams(dimension_semantics=sem, vmem_limit_bytes=VMEM_LIMIT)


def _const_spec(shape):
    return pl.BlockSpec(shape, lambda *_: (0,) * len(shape), pipeline_mode=pl.Buffered(1))


def _ffn_kernel(x_ref, g_ref, wg_ref, wu_ref, wd_ref, gf_ref, *rest, final_norm):
    o_ref = rest[-1]
    x = x_ref[...]
    h = _rms(x, g_ref[...]).astype(BF16)
    gate = _dot(h, wg_ref[...])
    up = _dot(h, wu_ref[...])
    act = (gate / (1.0 + jnp.exp(-gate)) * up).astype(BF16)
    y = x + 0.5 * _dot(act, wd_ref[...])
    if final_norm:
        y = _rms(y, gf_ref[...])
    o_ref[...] = y


def _ffn(x, g, w_up, w_down, g_final, *, final_norm, rows=None, in_row0=0, out_rows=None, out_row0=0,
         out_prev=None):
    tm = TOKEN_TILE
    rows = x.shape[0] if rows is None else rows
    out_rows = rows if out_rows is None else out_rows
    in0, out0 = in_row0 // tm, out_row0 // tm
    in_specs = [
        pl.BlockSpec((tm, D_MODEL), lambda i: (in0 + i, 0)),
        _const_spec((1, D_MODEL)),
        pl.BlockSpec((D_MODEL, D_FF), lambda i: (0, 0), pipeline_mode=pl.Buffered(1)),
        pl.BlockSpec((D_MODEL, D_FF), lambda i: (0, 1), pipeline_mode=pl.Buffered(1)),
        _const_spec((D_FF, D_MODEL)),
        _const_spec((1, D_MODEL)),
    ]
    args = [x, g, w_up, w_up, w_down, g_final]
    aliases = {}
    if out_prev is not None:
        in_specs.append(pl.BlockSpec(memory_space=pl.ANY))
        args.append(out_prev)
        aliases = {len(args) - 1: 0}
    return pl.pallas_call(
        functools.partial(_ffn_kernel, final_norm=final_norm),
        grid=(rows // tm,),
        in_specs=in_specs,
        out_specs=pl.BlockSpec((tm, D_MODEL), lambda i: (out0 + i, 0)),
        out_shape=jax.ShapeDtypeStruct((out_rows, D_MODEL), F32),
        input_output_aliases=aliases,
        compiler_params=_params(("parallel",)),
        name="ffn",
    )(*args)


def _pre_kernel(x_ref, g_ref, win_ref, wgate_ref, cos_ref, sin_ref, dft_ref,
                q_ref, kt_ref, v_ref, fab_ref, gate_ref):
    h = _rms(x_ref[...], g_ref[...]).astype(BF16)
    proj = _dot(h, win_ref[...])
    cos = cos_ref[...]
    sin = sin_ref[...]
    lane = lax.broadcasted_iota(jnp.int32, cos.shape, 1)
    low_half = (lane % HEAD_DIM) < (HEAD_DIM // 2)
    for j in range(2 * N_HEADS):
        t = proj[:, j * LANES:(j + 1) * LANES]
        rot = jnp.where(low_half,
                        pltpu.roll(t, LANES - HEAD_DIM // 2, axis=1),
                        pltpu.roll(t, HEAD_DIM // 2, axis=1))
        roped = t * cos + rot * sin
        if j < N_HEADS:
            q_ref[:, j * LANES:(j + 1) * LANES] = (roped * Q_SCALE).astype(BF16)
        else:
            kt_ref[(j - N_HEADS) * LANES:(j - N_HEADS + 1) * LANES, :] = roped.T.astype(BF16)
    v_ref[...] = proj[:, 2 * QK_W:QKV_W].astype(BF16)
    dft = dft_ref[...]
    for grp in range(N_FGROUPS):
        u = proj[:, QKV_W + grp * FGROUP_DIM:QKV_W + (grp + 1) * FGROUP_DIM].astype(BF16)
        ab = _dot(u, dft)
        fab_ref[:, grp * FGROUP_DIM:(grp + 1) * FGROUP_DIM] = ab[:, :FGROUP_DIM].astype(BF16)
        fab_ref[:, FOURIER_W + grp * FGROUP_DIM:FOURIER_W + (grp + 1) * FGROUP_DIM] = (
            ab[:, FGROUP_DIM:].astype(BF16))
    z = _dot(h, wgate_ref[...])
    gate_ref[...] = (1.0 / (1.0 + jnp.exp(-z))).astype(BF16)


def _pre(x, g, w_in, w_gate, cos, sin, dft, *, n_prompt, s_prompt, s_sample):
    t = x.shape[0]
    tm = TOKEN_TILE
    n_prompt_tiles = n_prompt // tm

    def pos_map(i):
        p = jnp.where(i < n_prompt_tiles, i % (s_prompt // tm), (i - n_prompt_tiles) % (s_sample // tm))
        return (p, 0)

    return pl.pallas_call(
        _pre_kernel,
        grid=(t // tm,),
        in_specs=[
            pl.BlockSpec((tm, D_MODEL), lambda i: (i, 0)),
            _const_spec((1, D_MODEL)),
            _const_spec((D_MODEL, IN_W)),
            _const_spec((D_MODEL, 2 * D_MODEL)),
            pl.BlockSpec((tm, LANES), pos_map),
            pl.BlockSpec((tm, LANES), pos_map),
            _const_spec((FGROUP_DIM, 2 * FGROUP_DIM)),
        ],
        out_specs=[
            pl.BlockSpec((tm, QK_W), lambda i: (i, 0)),
            pl.BlockSpec((QK_W, tm), lambda i: (0, i)),
            pl.BlockSpec((tm, ATTN_W), lambda i: (i, 0)),
            pl.BlockSpec((tm, 2 * FOURIER_W), lambda i: (i, 0)),
            pl.BlockSpec((tm, 2 * D_MODEL), lambda i: (i, 0)),
        ],
        out_shape=[
            jax.ShapeDtypeStruct((t, QK_W), BF16),
            jax.ShapeDtypeStruct((QK_W, t), BF16),
            jax.ShapeDtypeStruct((t, ATTN_W), BF16),
            jax.ShapeDtypeStruct((t, 2 * FOURIER_W), BF16),
            jax.ShapeDtypeStruct((t, 2 * D_MODEL), BF16),
        ],
        compiler_params=_params(("parallel",)),
        name="mixer_in",
    )(x, g, w_in, w_gate, cos, sin, dft)


def _attn_kernel(lamv_ref, q_ref, kt_ref, v_ref, gsub_ref, *rest, lam_init, n_items):
    o_ref = rest[-9]
    slots = ((rest[-8:-6], rest[-4:-2]), (rest[-6:-4], rest[-2:]))
    t = pl.program_id(0)

    def scores(slot):
        s_refs, m_refs = slot
        q = q_ref[...]
        kt = kt_ref[...]
        lane = lax.broadcasted_iota(jnp.int32, q.shape, 1)
        zero = jnp.zeros_like(q)
        for comp, qc in enumerate((jnp.where(lane < HEAD_DIM, q, zero), jnp.where(lane >= HEAD_DIM, q, zero))):
            s = _dot(qc, kt)
            s_refs[comp][...] = s
            m_refs[comp][...] = jnp.max(s, axis=-1, keepdims=True)

    def finish(slot):
        s_refs, m_refs = slot
        lamv = lamv_ref[...]
        lam = (jnp.exp(jnp.sum(lamv[0:1] * lamv[1:2], axis=-1, keepdims=True))
               - jnp.exp(jnp.sum(lamv[2:3] * lamv[3:4], axis=-1, keepdims=True)) + lam_init)
        v = v_ref[...]
        v1 = jnp.concatenate([v, jnp.ones_like(v)], axis=1)
        r = [_dot(jnp.exp2(s_refs[comp][...] - m_refs[comp][...]).astype(BF16), v1) for comp in range(2)]
        o = r[0][:, :V_DIM] / r[0][:, V_DIM:V_DIM + 1] - lam * (r[1][:, :V_DIM] / r[1][:, V_DIM:V_DIM + 1])
        o_ref[...] = (_rms(o, gsub_ref[...]) * (1.0 - lam_init)).astype(BF16)

    middle = jnp.logical_and(t > 0, t < n_items)

    @pl.when(t == 0)
    def _():
        scores(slots[0])

    for parity in range(2):
        @pl.when(jnp.logical_and(middle, t % 2 == parity))
        def _():
            scores(slots[parity])
            finish(slots[1 - parity])

    @pl.when(t == n_items)
    def _():
        finish(slots[(n_items - 1) % 2])


def _attn(q, kt, v, lamv, gsub, o_prev, *, row0, batch, seq, lam_init):
    t = q.shape[0]
    tq = ATTN_TQ
    b0 = row0 // seq
    q0 = row0 // tq
    nq = seq // tq
    n_items = batch * N_HEADS * nq

    def item(step):
        return step // (N_HEADS * nq), (step // nq) % N_HEADS, step % nq

    def scored(step):
        return item(jnp.minimum(step, n_items - 1))

    def finished(step):
        return item(jnp.maximum(step - 1, 0))

    def q_map(step):
        b, h, i = scored(step)
        return (q0 + b * nq + i, h)

    def kt_map(step):
        b, h, _ = scored(step)
        return (h, b0 + b)

    def v_map(step):
        b, h, _ = finished(step)
        return (b0 + b, h)

    def o_map(step):
        b, h, i = finished(step)
        return (q0 + b * nq + i, h)

    in_specs = [
        _const_spec((4, HEAD_DIM)),
        pl.BlockSpec((tq, LANES), q_map),
        pl.BlockSpec((LANES, seq), kt_map),
        pl.BlockSpec((seq, V_DIM), v_map),
        _const_spec((1, V_DIM)),
    ]
    args = [lamv, q, kt, v, gsub]
    aliases = {}
    if o_prev is not None:
        in_specs.append(pl.BlockSpec(memory_space=pl.ANY))
        args.append(o_prev)
        aliases = {len(args) - 1: 0}
    return pl.pallas_call(
        functools.partial(_attn_kernel, lam_init=lam_init, n_items=n_items),
        grid=(n_items + 1,),
        in_specs=in_specs,
        out_specs=pl.BlockSpec((tq, V_DIM), o_map),
        out_shape=jax.ShapeDtypeStruct((t, ATTN_W), BF16),
        scratch_shapes=[pltpu.VMEM((tq, seq), F32)] * 4 + [pltpu.VMEM((tq, 1), F32)] * 4,
        input_output_aliases=aliases,
        compiler_params=_params(("arbitrary",)),
        name=f"diff_attn_s{seq}",
    )(*args)


def _seqdft_kernel(c_ref, s_ref, fab_ref, *rest, scale, seq):
    o_ref = rest[-1]
    r0 = pl.multiple_of(pl.program_id(2) * seq, seq)
    out = (_dot(c_ref[...], fab_ref[pl.ds(r0, seq), :FOURIER_W])
           - _dot(s_ref[...], fab_ref[pl.ds(r0, seq), FOURIER_W:]))
    o_ref[...] = (out * scale).astype(BF16)


def _seqdft(fab, ctab, stab, f_prev, *, row0, batch, seq):
    t = fab.shape[0]
    tq = DFT_TQ
    nb = min(batch, DFT_ROWS // seq)
    g0 = row0 // (nb * seq)
    q0 = row0 // tq
    nq = seq // tq
    in_specs = [
        pl.BlockSpec((tq, seq), lambda g, i, j: (i, 0)),
        pl.BlockSpec((tq, seq), lambda g, i, j: (i, 0)),
        pl.BlockSpec((nb * seq, 2 * FOURIER_W), lambda g, i, j: (g0 + g, 0)),
    ]
    args = [ctab, stab, fab]
    aliases = {}
    if f_prev is not None:
        in_specs.append(pl.BlockSpec(memory_space=pl.ANY))
        args.append(f_prev)
        aliases = {len(args) - 1: 0}
    return pl.pallas_call(
        functools.partial(_seqdft_kernel, scale=1.0 / math.sqrt(seq * FGROUP_DIM), seq=seq),
        grid=(batch // nb, nq, nb),
        in_specs=in_specs,
        out_specs=pl.BlockSpec((tq, FOURIER_W), lambda g, i, j: (q0 + (g * nb + j) * nq + i, 0)),
        out_shape=jax.ShapeDtypeStruct((t, FOURIER_W), BF16),
        input_output_aliases=aliases,
        compiler_params=_params(("parallel", "parallel", "arbitrary")),
        name=f"seq_dft_s{seq}",
    )(*args)


def _post_kernel(x_ref, o_ref, f_ref, gate_ref, wpa_ref, wpf_ref, wo_ref, out_ref):
    br_a = _dot(o_ref[...], wpa_ref[...])
    br_f = _dot(f_ref[...], wpf_ref[...])
    merged = (gate_ref[:, :D_MODEL].astype(F32) * br_a + gate_ref[:, D_MODEL:].astype(F32) * br_f)
    out_ref[...] = x_ref[...] + _dot(merged.astype(BF16), wo_ref[...])


def _post(x, o, f, gate, w_pa, w_pf, w_o):
    t = x.shape[0]
    tm = TOKEN_TILE
    return pl.pallas_call(
        _post_kernel,
        grid=(t // tm,),
        in_specs=[
            pl.BlockSpec((tm, D_MODEL), lambda i: (i, 0)),
            pl.BlockSpec((tm, ATTN_W), lambda i: (i, 0)),
            pl.BlockSpec((tm, FOURIER_W), lambda i: (i, 0)),
            pl.BlockSpec((tm, 2 * D_MODEL), lambda i: (i, 0)),
            _const_spec((ATTN_W, D_MODEL)),
            _const_spec((FOURIER_W, D_MODEL)),
            _const_spec((D_MODEL, D_MODEL)),
        ],
        out_specs=pl.BlockSpec((tm, D_MODEL), lambda i: (i, 0)),
        out_shape=jax.ShapeDtypeStruct((t, D_MODEL), F32),
        compiler_params=_params(("parallel",)),
        name="mixer_out",
    )(x, o, f, gate, w_pa, w_pf, w_o)


def _rope_tables(seq):
    inv = 1.0 / (ROPE_THETA ** (jnp.arange(0, HEAD_DIM, 2, dtype=F32) / HEAD_DIM))
    ang = jnp.arange(seq, dtype=F32)[:, None] * inv[None, :]
    ang = jnp.concatenate([ang, ang, ang, ang], axis=-1)
    lane = jnp.arange(LANES)
    sign = jnp.where((lane % HEAD_DIM) < HEAD_DIM // 2, -1.0, 1.0).astype(F32)
    return jnp.cos(ang), jnp.sin(ang) * sign[None, :]


def _dft_tables(rows, cols, n):
    j = jnp.arange(rows, dtype=jnp.int32)
    k = jnp.arange(cols, dtype=jnp.int32)
    ang = ((j[:, None] * k[None, :]) % n).astype(F32) * (2.0 * math.pi / n)
    return jnp.cos(ang), jnp.sin(ang)


def _seq_tables_kernel(ca_ref, sa_ref, cb_ref, sb_ref, c_ref, s_ref):
    cb, sb = cb_ref[...], sb_ref[...]
    na = c_ref.shape[0] // DFT_SPLIT
    for a in range(na):
        row = pl.ds(pl.program_id(0) * na + a, 1)
        ca, sa = ca_ref[row, :], sa_ref[row, :]
        c_ref[a * DFT_SPLIT:(a + 1) * DFT_SPLIT, :] = (ca * cb - sa * sb).astype(BF16)
        s_ref[a * DFT_SPLIT:(a + 1) * DFT_SPLIT, :] = (sa * cb + ca * sb).astype(BF16)


def _seq_tables(n):
    ca, sa = _dft_tables(n // DFT_SPLIT, n, n // DFT_SPLIT)
    cb, sb = _dft_tables(DFT_SPLIT, n, n)
    rows = DFT_TQ
    return pl.pallas_call(
        _seq_tables_kernel,
        grid=(n // rows,),
        in_specs=[_const_spec((n // DFT_SPLIT, n))] * 2 + [_const_spec((DFT_SPLIT, n))] * 2,
        out_specs=[pl.BlockSpec((rows, n), lambda i: (i, 0))] * 2,
        out_shape=[jax.ShapeDtypeStruct((n, n), BF16)] * 2,
        compiler_params=_params(("parallel",)),
        name=f"dft_tables_{n}",
    )(ca, sa, cb, sb)


def kernel(x_prompt, x_sample, g_ff1, w_ff1_up, w_ff1_down, g_mix, w_in, lam_q1, lam_k1, lam_q2, lam_k2,
           g_sub, w_pa, w_pf, w_gate, w_o, g_ff2, w_ff2_up, w_ff2_down, g_final):
    bp, sp, _ = x_prompt.shape
    bs, ss, _ = x_sample.shape
    n_prompt = bp * sp
    assert n_prompt % ss == 0 and sp <= ss
    n_sample = bs * ss
    n_tok = n_prompt + n_sample

    cos, sin = _rope_tables(ss)
    cg, sg = _dft_tables(FGROUP_DIM, FGROUP_DIM, FGROUP_DIM)
    dft = jnp.concatenate([cg, sg], axis=1).astype(BF16)
    seq_tabs = {s: _seq_tables(s) for s in {sp, ss}}

    row = lambda v: v.reshape(1, -1)
    gf = row(g_final)
    x = None
    for l in range(DEPTH):
        lam_init = 0.8 - 0.6 * math.exp(-0.3 * l)
        ff1 = (row(g_ff1[l]), w_ff1_up[l].astype(BF16), w_ff1_down[l].astype(BF16), gf)
        if l == 0:
            x = _ffn(x_prompt.reshape(n_prompt, D_MODEL), *ff1, final_norm=False, out_rows=n_tok)
            x = _ffn(x_sample.reshape(n_sample, D_MODEL), *ff1, final_norm=False, out_rows=n_tok,
                     out_row0=n_prompt, out_prev=x)
        else:
            x = _ffn(x, *ff1, final_norm=False)
        q, kt, v, fab, gate = _pre(x, row(g_mix[l]), w_in[l].astype(BF16), w_gate[l].astype(BF16),
                                   cos, sin, dft, n_prompt=n_prompt, s_prompt=sp, s_sample=ss)
        lamv = jnp.stack([lam_q1[l], lam_k1[l], lam_q2[l], lam_k2[l]])
        o = _attn(q, kt, v, lamv, row(g_sub[l]), None, row0=0, batch=bp, seq=sp, lam_init=lam_init)
        o = _attn(q, kt, v, lamv, row(g_sub[l]), o, row0=n_prompt, batch=bs, seq=ss, lam_init=lam_init)
        f = _seqdft(fab, *seq_tabs[sp], None, row0=0, batch=bp, seq=sp)
        f = _seqdft(fab, *seq_tabs[ss], f, row0=n_prompt, batch=bs, seq=ss)
        x = _post(x, o, f, gate, w_pa[l].astype(BF16), w_pf[l].astype(BF16), w_o[l].astype(BF16))
        ff2 = (row(g_ff2[l]), w_ff2_up[l].astype(BF16), w_ff2_down[l].astype(BF16), gf)
        if l < DEPTH - 1:
            x = _ffn(x, *ff2, final_norm=False)
    y_prompt = _ffn(x, *ff2, final_norm=True, rows=n_prompt)
    y_sample = _ffn(x, *ff2, final_norm=True, rows=n_sample, in_row0=n_prompt)
    return y_prompt.reshape(bp, sp, D_MODEL), y_sample.reshape(bs, ss, D_MODEL)
```

```python
import functools
import math

import jax
import jax.numpy as jnp
from jax import lax
from jax.experimental import pallas as pl
from jax.experimental.pallas import tpu as pltpu

D_MODEL = 1024
DEPTH = 4
N_HEADS = 4
HEAD_DIM = 64
V_DIM = 2 * HEAD_DIM
QK_W = N_HEADS * 2 * HEAD_DIM
ATTN_W = N_HEADS * V_DIM
N_FGROUPS = 4
FGROUP_DIM = 128
FOURIER_W = N_FGROUPS * FGROUP_DIM
IN_W = 2 * QK_W + ATTN_W + FOURIER_W
D_FF = 2816
ROPE_THETA = 10000.0
EPS = 1e-6

LANES = 128
QKV_W = 2 * QK_W + ATTN_W
Q_SCALE = HEAD_DIM ** -0.5 * math.log2(math.e)
VMEM_LIMIT = 56 * 1024 * 1024
TOKEN_TILE = 512
ATTN_TILE = 256 * 4096
DFT_TQ = 512
DFT_ROWS = 8192
DFT_SPLIT = 64

F32 = jnp.float32
BF16 = jnp.bfloat16


def _rms(x, g):
    return x * lax.rsqrt(jnp.mean(x * x, axis=-1, keepdims=True) + EPS) * g


def _dot(a, b):
    return jnp.dot(a, b, preferred_element_type=F32)


def _params(sem):
    return pltpu.CompilerParams(dimension_semantics=sem, vmem_limit_bytes=VMEM_LIMIT)


def _const_spec(shape):
    return pl.BlockSpec(shape, lambda *_: (0,) * len(shape), pipeline_mode=pl.Buffered(1))


def _ffn_kernel(x_ref, g_ref, wg_ref, wu_ref, wd_ref, gf_ref, *rest, final_norm):
    o_ref = rest[-1]
    x = x_ref[...]
    h = _rms(x, g_ref[...]).astype(BF16)
    gate = _dot(h, wg_ref[...])
    up = _dot(h, wu_ref[...])
    act = (gate / (1.0 + jnp.exp(-gate)) * up).astype(BF16)
    y = x + 0.5 * _dot(act, wd_ref[...])
    if final_norm:
        y = _rms(y, gf_ref[...])
    o_ref[...] = y


def _ffn(x, g, w_up, w_down, g_final, *, final_norm, rows=None, in_row0=0, out_rows=None, out_row0=0,
         out_prev=None):
    tm = TOKEN_TILE
    rows = x.shape[0] if rows is None else rows
    out_rows = rows if out_rows is None else out_rows
    in0, out0 = in_row0 // tm, out_row0 // tm
    in_specs = [
        pl.BlockSpec((tm, D_MODEL), lambda i: (in0 + i, 0)),
        _const_spec((1, D_MODEL)),
        pl.BlockSpec((D_MODEL, D_FF), lambda i: (0, 0), pipeline_mode=pl.Buffered(1)),
        pl.BlockSpec((D_MODEL, D_FF), lambda i: (0, 1), pipeline_mode=pl.Buffered(1)),
        _const_spec((D_FF, D_MODEL)),
        _const_spec((1, D_MODEL)),
    ]
    args = [x, g, w_up, w_up, w_down, g_final]
    aliases = {}
    if out_prev is not None:
        in_specs.append(pl.BlockSpec(memory_space=pl.ANY))
        args.append(out_prev)
        aliases = {len(args) - 1: 0}
    return pl.pallas_call(
        functools.partial(_ffn_kernel, final_norm=final_norm),
        grid=(rows // tm,),
        in_specs=in_specs,
        out_specs=pl.BlockSpec((tm, D_MODEL), lambda i: (out0 + i, 0)),
        out_shape=jax.ShapeDtypeStruct((out_rows, D_MODEL), F32),
        input_output_aliases=aliases,
        compiler_params=_params(("parallel",)),
        name="ffn",
    )(*args)


def _pre_kernel(x_ref, g_ref, win_ref, wgate_ref, cos_ref, sin_ref, dft_ref,
                q_ref, kt_ref, v_ref, fab_ref, gate_ref):
    h = _rms(x_ref[...], g_ref[...]).astype(BF16)
    proj = _dot(h, win_ref[...])
    cos = cos_ref[...]
    sin = sin_ref[...]
    lane = lax.broadcasted_iota(jnp.int32, cos.shape, 1)
    low_half = (lane % HEAD_DIM) < (HEAD_DIM // 2)
    for j in range(2 * N_HEADS):
        t = proj[:, j * LANES:(j + 1) * LANES]
        rot = jnp.where(low_half,
                        pltpu.roll(t, LANES - HEAD_DIM // 2, axis=1),
                        pltpu.roll(t, HEAD_DIM // 2, axis=1))
        roped = t * cos + rot * sin
        if j < N_HEADS:
            q_ref[:, j * LANES:(j + 1) * LANES] = (roped * Q_SCALE).astype(BF16)
        else:
            kt_ref[(j - N_HEADS) * LANES:(j - N_HEADS + 1) * LANES, :] = roped.T.astype(BF16)
    v_ref[...] = proj[:, 2 * QK_W:QKV_W].astype(BF16)
    dft = dft_ref[...]
    for grp in range(N_FGROUPS):
        u = proj[:, QKV_W + grp * FGROUP_DIM:QKV_W + (grp + 1) * FGROUP_DIM].astype(BF16)
        ab = _dot(u, dft)
        fab_ref[:, grp * FGROUP_DIM:(grp + 1) * FGROUP_DIM] = ab[:, :FGROUP_DIM].astype(BF16)
        fab_ref[:, FOURIER_W + grp * FGROUP_DIM:FOURIER_W + (grp + 1) * FGROUP_DIM] = (
            ab[:, FGROUP_DIM:].astype(BF16))
    z = _dot(h, wgate_ref[...])
    gate_ref[...] = (1.0 / (1.0 + jnp.exp(-z))).astype(BF16)


def _pre(x, g, w_in, w_gate, cos, sin, dft, *, n_prompt, s_prompt, s_sample):
    t = x.shape[0]
    tm = TOKEN_TILE
    n_prompt_tiles = n_prompt // tm

    def pos_map(i):
        p = jnp.where(i < n_prompt_tiles, i % (s_prompt // tm), (i - n_prompt_tiles) % (s_sample // tm))
        return (p, 0)

    return pl.pallas_call(
        _pre_kernel,
        grid=(t // tm,),
        in_specs=[
            pl.BlockSpec((tm, D_MODEL), lambda i: (i, 0)),
            _const_spec((1, D_MODEL)),
            _const_spec((D_MODEL, IN_W)),
            _const_spec((D_MODEL, 2 * D_MODEL)),
            pl.BlockSpec((tm, LANES), pos_map),
            pl.BlockSpec((tm, LANES), pos_map),
            _const_spec((FGROUP_DIM, 2 * FGROUP_DIM)),
        ],
        out_specs=[
            pl.BlockSpec((tm, QK_W), lambda i: (i, 0)),
            pl.BlockSpec((QK_W, tm), lambda i: (0, i)),
            pl.BlockSpec((tm, ATTN_W), lambda i: (i, 0)),
            pl.BlockSpec((tm, 2 * FOURIER_W), lambda i: (i, 0)),
            pl.BlockSpec((tm, 2 * D_MODEL), lambda i: (i, 0)),
        ],
        out_shape=[
            jax.ShapeDtypeStruct((t, QK_W), BF16),
            jax.ShapeDtypeStruct((QK_W, t), BF16),
            jax.ShapeDtypeStruct((t, ATTN_W), BF16),
            jax.ShapeDtypeStruct((t, 2 * FOURIER_W), BF16),
            jax.ShapeDtypeStruct((t, 2 * D_MODEL), BF16),
        ],
        compiler_params=_params(("parallel",)),
        name="mixer_in",
    )(x, g, w_in, w_gate, cos, sin, dft)


def _attn_kernel(lamv_ref, q_ref, kt_ref, v_ref, gsub_ref, *rest, lam_init, n_items):
    o_ref = rest[-9]
    slots = ((rest[-8:-6], rest[-4:-2]), (rest[-6:-4], rest[-2:]))
    t = pl.program_id(0)

    def scores(slot):
        s_refs, m_refs = slot
        q = q_ref[...]
        kt = kt_ref[...]
        lane = lax.broadcasted_iota(jnp.int32, q.shape, 1)
        zero = jnp.zeros_like(q)
        for comp, qc in enumerate((jnp.where(lane < HEAD_DIM, q, zero), jnp.where(lane >= HEAD_DIM, q, zero))):
            s = _dot(qc, kt)
            s_refs[comp][...] = s
            m_refs[comp][...] = jnp.max(s, axis=-1, keepdims=True)

    def finish(slot):
        s_refs, m_refs = slot
        lamv = lamv_ref[...]
        lam = (jnp.exp(jnp.sum(lamv[0:1] * lamv[1:2], axis=-1, keepdims=True))
               - jnp.exp(jnp.sum(lamv[2:3] * lamv[3:4], axis=-1, keepdims=True)) + lam_init)
        v = v_ref[...]
        v1 = jnp.concatenate([v, jnp.ones_like(v)], axis=1)
        r = [_dot(jnp.exp2(s_refs[comp][...] - m_refs[comp][...]).astype(BF16), v1) for comp in range(2)]
        o = r[0][:, :V_DIM] / r[0][:, V_DIM:V_DIM + 1] - lam * (r[1][:, :V_DIM] / r[1][:, V_DIM:V_DIM + 1])
        o_ref[...] = (_rms(o, gsub_ref[...]) * (1.0 - lam_init)).astype(BF16)

    middle = jnp.logical_and(t > 0, t < n_items)

    @pl.when(t == 0)
    def _():
        scores(slots[0])

    for parity in range(2):
        @pl.when(jnp.logical_and(middle, t % 2 == parity))
        def _():
            scores(slots[parity])
            finish(slots[1 - parity])

    @pl.when(t == n_items)
    def _():
        finish(slots[(n_items - 1) % 2])


def _attn(q, kt, v, lamv, gsub, o_prev, *, row0, batch, seq, lam_init):
    t = q.shape[0]
    tq = ATTN_TILE // seq
    b0 = row0 // seq
    q0 = row0 // tq
    nq = seq // tq
    n_items = batch * N_HEADS * nq

    def item(step):
        return step // (N_HEADS * nq), (step // nq) % N_HEADS, step % nq

    def scored(step):
        return item(jnp.minimum(step, n_items - 1))

    def finished(step):
        return item(jnp.maximum(step - 1, 0))

    def q_map(step):
        b, h, i = scored(step)
        return (q0 + b * nq + i, h)

    def kt_map(step):
        b, h, _ = scored(step)
        return (h, b0 + b)

    def v_map(step):
        b, h, _ = finished(step)
        return (b0 + b, h)

    def o_map(step):
        b, h, i = finished(step)
        return (q0 + b * nq + i, h)

    in_specs = [
        _const_spec((4, HEAD_DIM)),
        pl.BlockSpec((tq, LANES), q_map),
        pl.BlockSpec((LANES, seq), kt_map),
        pl.BlockSpec((seq, V_DIM), v_map),
        _const_spec((1, V_DIM)),
    ]
    args = [lamv, q, kt, v, gsub]
    aliases = {}
    if o_prev is not None:
        in_specs.append(pl.BlockSpec(memory_space=pl.ANY))
        args.append(o_prev)
        aliases = {len(args) - 1: 0}
    return pl.pallas_call(
        functools.partial(_attn_kernel, lam_init=lam_init, n_items=n_items),
        grid=(n_items + 1,),
        in_specs=in_specs,
        out_specs=pl.BlockSpec((tq, V_DIM), o_map),
        out_shape=jax.ShapeDtypeStruct((t, ATTN_W), BF16),
        scratch_shapes=[pltpu.VMEM((tq, seq), F32)] * 4 + [pltpu.VMEM((tq, 1), F32)] * 4,
        input_output_aliases=aliases,
        compiler_params=_params(("arbitrary",)),
        name=f"diff_attn_s{seq}",
    )(*args)


def _seqdft_kernel(c_ref, s_ref, fab_ref, *rest, scale, seq):
    o_ref = rest[-1]
    r0 = pl.multiple_of(pl.program_id(2) * seq, seq)
    out = (_dot(c_ref[...], fab_ref[pl.ds(r0, seq), :FOURIER_W])
           - _dot(s_ref[...], fab_ref[pl.ds(r0, seq), FOURIER_W:]))
    o_ref[...] = (out * scale).astype(BF16)


def _seqdft(fab, ctab, stab, f_prev, *, row0, batch, seq):
    t = fab.shape[0]
    tq = DFT_TQ
    nb = min(batch, DFT_ROWS // seq)
    g0 = row0 // (nb * seq)
    q0 = row0 // tq
    nq = seq // tq
    in_specs = [
        pl.BlockSpec((tq, seq), lambda g, i, j: (i, 0)),
        pl.BlockSpec((tq, seq), lambda g, i, j: (i, 0)),
        pl.BlockSpec((nb * seq, 2 * FOURIER_W), lambda g, i, j: (g0 + g, 0)),
    ]
    args = [ctab, stab, fab]
    aliases = {}
    if f_prev is not None:
        in_specs.append(pl.BlockSpec(memory_space=pl.ANY))
        args.append(f_prev)
        aliases = {len(args) - 1: 0}
    return pl.pallas_call(
        functools.partial(_seqdft_kernel, scale=1.0 / math.sqrt(seq * FGROUP_DIM), seq=seq),
        grid=(batch // nb, nq, nb),
        in_specs=in_specs,
        out_specs=pl.BlockSpec((tq, FOURIER_W), lambda g, i, j: (q0 + (g * nb + j) * nq + i, 0)),
        out_shape=jax.ShapeDtypeStruct((t, FOURIER_W), BF16),
        input_output_aliases=aliases,
        compiler_params=_params(("parallel", "parallel", "arbitrary")),
        name=f"seq_dft_s{seq}",
    )(*args)


def _post_kernel(x_ref, o_ref, f_ref, gate_ref, wpa_ref, wpf_ref, wo_ref, out_ref):
    br_a = _dot(o_ref[...], wpa_ref[...])
    br_f = _dot(f_ref[...], wpf_ref[...])
    merged = (gate_ref[:, :D_MODEL].astype(F32) * br_a + gate_ref[:, D_MODEL:].astype(F32) * br_f)
    out_ref[...] = x_ref[...] + _dot(merged.astype(BF16), wo_ref[...])


def _post(x, o, f, gate, w_pa, w_pf, w_o):
    t = x.shape[0]
    tm = TOKEN_TILE
    return pl.pallas_call(
        _post_kernel,
        grid=(t // tm,),
        in_specs=[
            pl.BlockSpec((tm, D_MODEL), lambda i: (i, 0)),
            pl.BlockSpec((tm, ATTN_W), lambda i: (i, 0)),
            pl.BlockSpec((tm, FOURIER_W), lambda i: (i, 0)),
            pl.BlockSpec((tm, 2 * D_MODEL), lambda i: (i, 0)),
            _const_spec((ATTN_W, D_MODEL)),
            _const_spec((FOURIER_W, D_MODEL)),
            _const_spec((D_MODEL, D_MODEL)),
        ],
        out_specs=pl.BlockSpec((tm, D_MODEL), lambda i: (i, 0)),
        out_shape=jax.ShapeDtypeStruct((t, D_MODEL), F32),
        compiler_params=_params(("parallel",)),
        name="mixer_out",
    )(x, o, f, gate, w_pa, w_pf, w_o)


def _rope_tables(seq):
    inv = 1.0 / (ROPE_THETA ** (jnp.arange(0, HEAD_DIM, 2, dtype=F32) / HEAD_DIM))
    ang = jnp.arange(seq, dtype=F32)[:, None] * inv[None, :]
    ang = jnp.concatenate([ang, ang, ang, ang], axis=-1)
    lane = jnp.arange(LANES)
    sign = jnp.where((lane % HEAD_DIM) < HEAD_DIM // 2, -1.0, 1.0).astype(F32)
    return jnp.cos(ang), jnp.sin(ang) * sign[None, :]


def _dft_tables(rows, cols, n):
    j = jnp.arange(rows, dtype=jnp.int32)
    k = jnp.arange(cols, dtype=jnp.int32)
    ang = ((j[:, None] * k[None, :]) % n).astype(F32) * (2.0 * math.pi / n)
    return jnp.cos(ang), jnp.sin(ang)


def _seq_tables_kernel(ca_ref, sa_ref, cb_ref, sb_ref, c_ref, s_ref):
    cb, sb = cb_ref[...], sb_ref[...]
    na = c_ref.shape[0] // DFT_SPLIT
    for a in range(na):
        row = pl.ds(pl.program_id(0) * na + a, 1)
        ca, sa = ca_ref[row, :], sa_ref[row, :]
        c_ref[a * DFT_SPLIT:(a + 1) * DFT_SPLIT, :] = (ca * cb - sa * sb).astype(BF16)
        s_ref[a * DFT_SPLIT:(a + 1) * DFT_SPLIT, :] = (sa * cb + ca * sb).astype(BF16)


def _seq_tables(n):
    ca, sa = _dft_tables(n // DFT_SPLIT, n, n // DFT_SPLIT)
    cb, sb = _dft_tables(DFT_SPLIT, n, n)
    rows = DFT_TQ
    return pl.pallas_call(
        _seq_tables_kernel,
        grid=(n // rows,),
        in_specs=[_const_spec((n // DFT_SPLIT, n))] * 2 + [_const_spec((DFT_SPLIT, n))] * 2,
        out_specs=[pl.BlockSpec((rows, n), lambda i: (i, 0))] * 2,
        out_shape=[jax.ShapeDtypeStruct((n, n), BF16)] * 2,
        compiler_params=_params(("parallel",)),
        name=f"dft_tables_{n}",
    )(ca, sa, cb, sb)


def kernel(x_prompt, x_sample, g_ff1, w_ff1_up, w_ff1_down, g_mix, w_in, lam_q1, lam_k1, lam_q2, lam_k2,
           g_sub, w_pa, w_pf, w_gate, w_o, g_ff2, w_ff2_up, w_ff2_down, g_final):
    bp, sp, _ = x_prompt.shape
    bs, ss, _ = x_sample.shape
    n_prompt = bp * sp
    assert n_prompt % ss == 0 and sp <= ss
    n_sample = bs * ss
    n_tok = n_prompt + n_sample

    cos, sin = _rope_tables(ss)
    cg, sg = _dft_tables(FGROUP_DIM, FGROUP_DIM, FGROUP_DIM)
    dft = jnp.concatenate([cg, sg], axis=1).astype(BF16)
    seq_tabs = {s: _seq_tables(s) for s in {sp, ss}}

    row = lambda v: v.reshape(1, -1)
    gf = row(g_final)
    x = None
    for l in range(DEPTH):
        lam_init = 0.8 - 0.6 * math.exp(-0.3 * l)
        ff1 = (row(g_ff1[l]), w_ff1_up[l].astype(BF16), w_ff1_down[l].astype(BF16), gf)
        if l == 0:
            x = _ffn(x_prompt.reshape(n_prompt, D_MODEL), *ff1, final_norm=False, out_rows=n_tok)
            x = _ffn(x_sample.reshape(n_sample, D_MODEL), *ff1, final_norm=False, out_rows=n_tok,
                     out_row0=n_prompt, out_prev=x)
        else:
            x = _ffn(x, *ff1, final_norm=False)
        q, kt, v, fab, gate = _pre(x, row(g_mix[l]), w_in[l].astype(BF16), w_gate[l].astype(BF16),
                                   cos, sin, dft, n_prompt=n_prompt, s_prompt=sp, s_sample=ss)
        lamv = jnp.stack([lam_q1[l], lam_k1[l], lam_q2[l], lam_k2[l]])
        o = _attn(q, kt, v, lamv, row(g_sub[l]), None, row0=0, batch=bp, seq=sp, lam_init=lam_init)
        o = _attn(q, kt, v, lamv, row(g_sub[l]), o, row0=n_prompt, batch=bs, seq=ss, lam_init=lam_init)
        f = _seqdft(fab, *seq_tabs[sp], None, row0=0, batch=bp, seq=sp)
        f = _seqdft(fab, *seq_tabs[ss], f, row0=n_prompt, batch=bs, seq=ss)
        x = _post(x, o, f, gate, w_pa[l].astype(BF16), w_pf[l].astype(BF16), w_o[l].astype(BF16))
        ff2 = (row(g_ff2[l]), w_ff2_up[l].astype(BF16), w_ff2_down[l].astype(BF16), gf)
        if l < DEPTH - 1:
            x = _ffn(x, *ff2, final_norm=False)
    y_prompt = _ffn(x, *ff2, final_norm=True, rows=n_prompt)
    y_sample = _ffn(x, *ff2, final_norm=True, rows=n_sample, in_row0=n_prompt)
    return y_prompt.reshape(bp, sp, D_MODEL), y_sample.reshape(bs, ss, D_MODEL)
```

```python
import functools
import math

import jax
import jax.numpy as jnp
from jax import lax
from jax.experimental import pallas as pl
from jax.experimental.pallas import tpu as pltpu

D_MODEL = 1024
DEPTH = 4
N_HEADS = 4
HEAD_DIM = 64
V_DIM = 2 * HEAD_DIM
QK_W = N_HEADS * 2 * HEAD_DIM
ATTN_W = N_HEADS * V_DIM
N_FGROUPS = 4
FGROUP_DIM = 128
FOURIER_W = N_FGROUPS * FGROUP_DIM
IN_W = 2 * QK_W + ATTN_W + FOURIER_W
D_FF = 2816
ROPE_THETA = 10000.0
EPS = 1e-6

LANES = 128
QKV_W = 2 * QK_W + ATTN_W
Q_SCALE = HEAD_DIM ** -0.5 * math.log2(math.e)
VMEM_LIMIT = 56 * 1024 * 1024
TOKEN_TILE = 512
ATTN_TILE = 256 * 4096
DFT_TQ = 512
DFT_ROWS = 8192
DFT_SPLIT = 64

F32 = jnp.float32
BF16 = jnp.bfloat16


def _rms(x, g):
    return x * lax.rsqrt(jnp.mean(x * x, axis=-1, keepdims=True) + EPS) * g


def _dot(a, b):
    return jnp.dot(a, b, preferred_element_type=F32)


def _params(sem):
    return pltpu.CompilerParams(dimension_semantics=sem, vmem_limit_bytes=VMEM_LIMIT)


def _const_spec(shape):
    return pl.BlockSpec(shape, lambda *_: (0,) * len(shape), pipeline_mode=pl.Buffered(1))


def _layer_spec(shape, layer, col_block=0):
    return pl.BlockSpec((None,) + shape, lambda *_: (layer, 0, col_block), pipeline_mode=pl.Buffered(1))


def _ffn_kernel(x_ref, g_ref, wg_ref, wu_ref, wd_ref, gf_ref, *rest, final_norm):
    o_ref = rest[-1]
    x = x_ref[...]
    h = _rms(x, g_ref[...]).astype(BF16)
    gate = _dot(h, wg_ref[...])
    up = _dot(h, wu_ref[...])
    act = (gate / (1.0 + jnp.exp(-gate)) * up).astype(BF16)
    y = x + 0.5 * _dot(act, wd_ref[...])
    if final_norm:
        y = _rms(y, gf_ref[...])
    o_ref[...] = y


def _ffn(x, g, w_up, w_down, g_final, *, layer, final_norm, rows=None, in_row0=0, out_rows=None, out_row0=0,
         out_prev=None):
    tm = TOKEN_TILE
    rows = x.shape[0] if rows is None else rows
    out_rows = rows if out_rows is None else out_rows
    in0, out0 = in_row0 // tm, out_row0 // tm
    in_specs = [
        pl.BlockSpec((tm, D_MODEL), lambda i: (in0 + i, 0)),
        _const_spec((1, D_MODEL)),
        _layer_spec((D_MODEL, D_FF), layer, 0),
        _layer_spec((D_MODEL, D_FF), layer, 1),
        _layer_spec((D_FF, D_MODEL), layer),
        _const_spec((1, D_MODEL)),
    ]
    args = [x, g, w_up, w_up, w_down, g_final]
    aliases = {}
    if out_prev is not None:
        in_specs.append(pl.BlockSpec(memory_space=pl.ANY))
        args.append(out_prev)
        aliases = {len(args) - 1: 0}
    return pl.pallas_call(
        functools.partial(_ffn_kernel, final_norm=final_norm),
        grid=(rows // tm,),
        in_specs=in_specs,
        out_specs=pl.BlockSpec((tm, D_MODEL), lambda i: (out0 + i, 0)),
        out_shape=jax.ShapeDtypeStruct((out_rows, D_MODEL), F32),
        input_output_aliases=aliases,
        compiler_params=_params(("parallel",)),
        name="ffn",
    )(*args)


def _pre_kernel(x_ref, g_ref, win_ref, wgate_ref, cos_ref, sin_ref, dft_ref,
                q_ref, kt_ref, v_ref, fab_ref, gate_ref):
    h = _rms(x_ref[...], g_ref[...]).astype(BF16)
    proj = _dot(h, win_ref[...])
    cos = cos_ref[...]
    sin = sin_ref[...]
    lane = lax.broadcasted_iota(jnp.int32, cos.shape, 1)
    low_half = (lane % HEAD_DIM) < (HEAD_DIM // 2)
    for j in range(2 * N_HEADS):
        t = proj[:, j * LANES:(j + 1) * LANES]
        rot = jnp.where(low_half,
                        pltpu.roll(t, LANES - HEAD_DIM // 2, axis=1),
                        pltpu.roll(t, HEAD_DIM // 2, axis=1))
        roped = t * cos + rot * sin
        if j < N_HEADS:
            q_ref[:, j * LANES:(j + 1) * LANES] = (roped * Q_SCALE).astype(BF16)
        else:
            kt_ref[(j - N_HEADS) * LANES:(j - N_HEADS + 1) * LANES, :] = roped.T.astype(BF16)
    v_ref[...] = proj[:, 2 * QK_W:QKV_W].astype(BF16)
    dft = dft_ref[...]
    for grp in range(N_FGROUPS):
        u = proj[:, QKV_W + grp * FGROUP_DIM:QKV_W + (grp + 1) * FGROUP_DIM].astype(BF16)
        ab = _dot(u, dft)
        fab_ref[:, grp * FGROUP_DIM:(grp + 1) * FGROUP_DIM] = ab[:, :FGROUP_DIM].astype(BF16)
        fab_ref[:, FOURIER_W + grp * FGROUP_DIM:FOURIER_W + (grp + 1) * FGROUP_DIM] = (
            ab[:, FGROUP_DIM:].astype(BF16))
    z = _dot(h, wgate_ref[...])
    gate_ref[...] = (1.0 / (1.0 + jnp.exp(-z))).astype(BF16)


def _pre(x, g, w_in, w_gate, cos, sin, dft, *, layer, n_prompt, s_prompt, s_sample):
    t = x.shape[0]
    tm = TOKEN_TILE
    n_prompt_tiles = n_prompt // tm

    def pos_map(i):
        p = jnp.where(i < n_prompt_tiles, i % (s_prompt // tm), (i - n_prompt_tiles) % (s_sample // tm))
        return (p, 0)

    return pl.pallas_call(
        _pre_kernel,
        grid=(t // tm,),
        in_specs=[
            pl.BlockSpec((tm, D_MODEL), lambda i: (i, 0)),
            _const_spec((1, D_MODEL)),
            _layer_spec((D_MODEL, IN_W), layer),
            _layer_spec((D_MODEL, 2 * D_MODEL), layer),
            pl.BlockSpec((tm, LANES), pos_map),
            pl.BlockSpec((tm, LANES), pos_map),
            _const_spec((FGROUP_DIM, 2 * FGROUP_DIM)),
        ],
        out_specs=[
            pl.BlockSpec((tm, QK_W), lambda i: (i, 0)),
            pl.BlockSpec((QK_W, tm), lambda i: (0, i)),
            pl.BlockSpec((tm, ATTN_W), lambda i: (i, 0)),
            pl.BlockSpec((tm, 2 * FOURIER_W), lambda i: (i, 0)),
            pl.BlockSpec((tm, 2 * D_MODEL), lambda i: (i, 0)),
        ],
        out_shape=[
            jax.ShapeDtypeStruct((t, QK_W), BF16),
            jax.ShapeDtypeStruct((QK_W, t), BF16),
            jax.ShapeDtypeStruct((t, ATTN_W), BF16),
            jax.ShapeDtypeStruct((t, 2 * FOURIER_W), BF16),
            jax.ShapeDtypeStruct((t, 2 * D_MODEL), BF16),
        ],
        compiler_params=_params(("parallel",)),
        name="mixer_in",
    )(x, g, w_in, w_gate, cos, sin, dft)


def _attn_kernel(lamv_ref, q_ref, kt_ref, v_ref, gsub_ref, *rest, lam_init, n_items):
    o_ref = rest[-9]
    slots = ((rest[-8:-6], rest[-4:-2]), (rest[-6:-4], rest[-2:]))
    t = pl.program_id(0)

    def scores(slot):
        s_refs, m_refs = slot
        q = q_ref[...]
        kt = kt_ref[...]
        lane = lax.broadcasted_iota(jnp.int32, q.shape, 1)
        zero = jnp.zeros_like(q)
        for comp, qc in enumerate((jnp.where(lane < HEAD_DIM, q, zero), jnp.where(lane >= HEAD_DIM, q, zero))):
            s = _dot(qc, kt)
            s_refs[comp][...] = s
            m_refs[comp][...] = jnp.max(s, axis=-1, keepdims=True)

    def finish(slot):
        s_refs, m_refs = slot
        lamv = lamv_ref[...]
        lam = (jnp.exp(jnp.sum(lamv[0:1] * lamv[1:2], axis=-1, keepdims=True))
               - jnp.exp(jnp.sum(lamv[2:3] * lamv[3:4], axis=-1, keepdims=True)) + lam_init)
        v = v_ref[...]
        v1 = jnp.concatenate([v, jnp.ones_like(v)], axis=1)
        r = [_dot(jnp.exp2(s_refs[comp][...] - m_refs[comp][...]).astype(BF16), v1) for comp in range(2)]
        o = r[0][:, :V_DIM] / r[0][:, V_DIM:V_DIM + 1] - lam * (r[1][:, :V_DIM] / r[1][:, V_DIM:V_DIM + 1])
        o_ref[...] = (_rms(o, gsub_ref[...]) * (1.0 - lam_init)).astype(BF16)

    middle = jnp.logical_and(t > 0, t < n_items)

    @pl.when(t == 0)
    def _():
        scores(slots[0])

    for parity in range(2):
        @pl.when(jnp.logical_and(middle, t % 2 == parity))
        def _():
            scores(slots[parity])
            finish(slots[1 - parity])

    @pl.when(t == n_items)
    def _():
        finish(slots[(n_items - 1) % 2])


def _attn(q, kt, v, lamv, gsub, o_prev, *, row0, batch, seq, lam_init):
    t = q.shape[0]
    tq = ATTN_TILE // seq
    b0 = row0 // seq
    q0 = row0 // tq
    nq = seq // tq
    n_items = batch * N_HEADS * nq

    def item(step):
        return step // (N_HEADS * nq), (step // nq) % N_HEADS, step % nq

    def scored(step):
        return item(jnp.minimum(step, n_items - 1))

    def finished(step):
        return item(jnp.maximum(step - 1, 0))

    def q_map(step):
        b, h, i = scored(step)
        return (q0 + b * nq + i, h)

    def kt_map(step):
        b, h, _ = scored(step)
        return (h, b0 + b)

    def v_map(step):
        b, h, _ = finished(step)
        return (b0 + b, h)

    def o_map(step):
        b, h, i = finished(step)
        return (q0 + b * nq + i, h)

    in_specs = [
        _const_spec((4, HEAD_DIM)),
        pl.BlockSpec((tq, LANES), q_map),
        pl.BlockSpec((LANES, seq), kt_map),
        pl.BlockSpec((seq, V_DIM), v_map),
        _const_spec((1, V_DIM)),
    ]
    args = [lamv, q, kt, v, gsub]
    aliases = {}
    if o_prev is not None:
        in_specs.append(pl.BlockSpec(memory_space=pl.ANY))
        args.append(o_prev)
        aliases = {len(args) - 1: 0}
    return pl.pallas_call(
        functools.partial(_attn_kernel, lam_init=lam_init, n_items=n_items),
        grid=(n_items + 1,),
        in_specs=in_specs,
        out_specs=pl.BlockSpec((tq, V_DIM), o_map),
        out_shape=jax.ShapeDtypeStruct((t, ATTN_W), BF16),
        scratch_shapes=[pltpu.VMEM((tq, seq), F32)] * 4 + [pltpu.VMEM((tq, 1), F32)] * 4,
        input_output_aliases=aliases,
        compiler_params=_params(("arbitrary",)),
        name=f"diff_attn_s{seq}",
    )(*args)


def _seqdft_kernel(c_ref, s_ref, fab_ref, *rest, scale, seq):
    o_ref = rest[-1]
    r0 = pl.multiple_of(pl.program_id(2) * seq, seq)
    out = (_dot(c_ref[...], fab_ref[pl.ds(r0, seq), :FOURIER_W])
           - _dot(s_ref[...], fab_ref[pl.ds(r0, seq), FOURIER_W:]))
    o_ref[...] = (out * scale).astype(BF16)


def _seqdft(fab, ctab, stab, f_prev, *, row0, batch, seq):
    t = fab.shape[0]
    tq = DFT_TQ
    nb = min(batch, DFT_ROWS // seq)
    g0 = row0 // (nb * seq)
    q0 = row0 // tq
    nq = seq // tq
    in_specs = [
        pl.BlockSpec((tq, seq), lambda g, i, j: (i, 0)),
        pl.BlockSpec((tq, seq), lambda g, i, j: (i, 0)),
        pl.BlockSpec((nb * seq, 2 * FOURIER_W), lambda g, i, j: (g0 + g, 0)),
    ]
    args = [ctab, stab, fab]
    aliases = {}
    if f_prev is not None:
        in_specs.append(pl.BlockSpec(memory_space=pl.ANY))
        args.append(f_prev)
        aliases = {len(args) - 1: 0}
    return pl.pallas_call(
        functools.partial(_seqdft_kernel, scale=1.0 / math.sqrt(seq * FGROUP_DIM), seq=seq),
        grid=(batch // nb, nq, nb),
        in_specs=in_specs,
        out_specs=pl.BlockSpec((tq, FOURIER_W), lambda g, i, j: (q0 + (g * nb + j) * nq + i, 0)),
        out_shape=jax.ShapeDtypeStruct((t, FOURIER_W), BF16),
        input_output_aliases=aliases,
        compiler_params=_params(("parallel", "parallel", "arbitrary")),
        name=f"seq_dft_s{seq}",
    )(*args)


def _post_kernel(x_ref, o_ref, f_ref, gate_ref, wpa_ref, wpf_ref, wo_ref, out_ref):
    br_a = _dot(o_ref[...], wpa_ref[...])
    br_f = _dot(f_ref[...], wpf_ref[...])
    merged = (gate_ref[:, :D_MODEL].astype(F32) * br_a + gate_ref[:, D_MODEL:].astype(F32) * br_f)
    out_ref[...] = x_ref[...] + _dot(merged.astype(BF16), wo_ref[...])


def _post(x, o, f, gate, w_pa, w_pf, w_o, *, layer):
    t = x.shape[0]
    tm = TOKEN_TILE
    return pl.pallas_call(
        _post_kernel,
        grid=(t // tm,),
        in_specs=[
            pl.BlockSpec((tm, D_MODEL), lambda i: (i, 0)),
            pl.BlockSpec((tm, ATTN_W), lambda i: (i, 0)),
            pl.BlockSpec((tm, FOURIER_W), lambda i: (i, 0)),
            pl.BlockSpec((tm, 2 * D_MODEL), lambda i: (i, 0)),
            _layer_spec((ATTN_W, D_MODEL), layer),
            _layer_spec((FOURIER_W, D_MODEL), layer),
            _layer_spec((D_MODEL, D_MODEL), layer),
        ],
        out_specs=pl.BlockSpec((tm, D_MODEL), lambda i: (i, 0)),
        out_shape=jax.ShapeDtypeStruct((t, D_MODEL), F32),
        compiler_params=_params(("parallel",)),
        name="mixer_out",
    )(x, o, f, gate, w_pa, w_pf, w_o)


def _rope_tables(seq):
    inv = 1.0 / (ROPE_THETA ** (jnp.arange(0, HEAD_DIM, 2, dtype=F32) / HEAD_DIM))
    ang = jnp.arange(seq, dtype=F32)[:, None] * inv[None, :]
    ang = jnp.concatenate([ang, ang, ang, ang], axis=-1)
    lane = jnp.arange(LANES)
    sign = jnp.where((lane % HEAD_DIM) < HEAD_DIM // 2, -1.0, 1.0).astype(F32)
    return jnp.cos(ang), jnp.sin(ang) * sign[None, :]


def _dft_tables(rows, cols, n):
    j = jnp.arange(rows, dtype=jnp.int32)
    k = jnp.arange(cols, dtype=jnp.int32)
    ang = ((j[:, None] * k[None, :]) % n).astype(F32) * (2.0 * math.pi / n)
    return jnp.cos(ang), jnp.sin(ang)


def _seq_tables_kernel(ca_ref, sa_ref, cb_ref, sb_ref, c_ref, s_ref):
    cb, sb = cb_ref[...], sb_ref[...]
    na = c_ref.shape[0] // DFT_SPLIT
    for a in range(na):
        row = pl.ds(pl.program_id(0) * na + a, 1)
        ca, sa = ca_ref[row, :], sa_ref[row, :]
        c_ref[a * DFT_SPLIT:(a + 1) * DFT_SPLIT, :] = (ca * cb - sa * sb).astype(BF16)
        s_ref[a * DFT_SPLIT:(a + 1) * DFT_SPLIT, :] = (sa * cb + ca * sb).astype(BF16)


def _seq_tables(n):
    ca, sa = _dft_tables(n // DFT_SPLIT, n, n // DFT_SPLIT)
    cb, sb = _dft_tables(DFT_SPLIT, n, n)
    rows = DFT_TQ
    return pl.pallas_call(
        _seq_tables_kernel,
        grid=(n // rows,),
        in_specs=[_const_spec((n // DFT_SPLIT, n))] * 2 + [_const_spec((DFT_SPLIT, n))] * 2,
        out_specs=[pl.BlockSpec((rows, n), lambda i: (i, 0))] * 2,
        out_shape=[jax.ShapeDtypeStruct((n, n), BF16)] * 2,
        compiler_params=_params(("parallel",)),
        name=f"dft_tables_{n}",
    )(ca, sa, cb, sb)


def kernel(x_prompt, x_sample, g_ff1, w_ff1_up, w_ff1_down, g_mix, w_in, lam_q1, lam_k1, lam_q2, lam_k2,
           g_sub, w_pa, w_pf, w_gate, w_o, g_ff2, w_ff2_up, w_ff2_down, g_final):
    bp, sp, _ = x_prompt.shape
    bs, ss, _ = x_sample.shape
    n_prompt = bp * sp
    assert n_prompt % ss == 0 and sp <= ss
    n_sample = bs * ss
    n_tok = n_prompt + n_sample

    cos, sin = _rope_tables(ss)
    cg, sg = _dft_tables(FGROUP_DIM, FGROUP_DIM, FGROUP_DIM)
    dft = jnp.concatenate([cg, sg], axis=1).astype(BF16)
    seq_tabs = {s: _seq_tables(s) for s in {sp, ss}}

    row = lambda v: v.reshape(1, -1)
    gf = row(g_final)
    w_ff1_up, w_ff1_down, w_ff2_up, w_ff2_down, w_in, w_gate, w_pa, w_pf, w_o = (
        w.astype(BF16) for w in (w_ff1_up, w_ff1_down, w_ff2_up, w_ff2_down, w_in, w_gate, w_pa, w_pf, w_o))
    x = None
    for l in range(DEPTH):
        lam_init = 0.8 - 0.6 * math.exp(-0.3 * l)
        ff1 = (row(g_ff1[l]), w_ff1_up, w_ff1_down, gf)
        if l == 0:
            x = _ffn(x_prompt.reshape(n_prompt, D_MODEL), *ff1, layer=l, final_norm=False, out_rows=n_tok)
            x = _ffn(x_sample.reshape(n_sample, D_MODEL), *ff1, layer=l, final_norm=False, out_rows=n_tok,
                     out_row0=n_prompt, out_prev=x)
        else:
            x = _ffn(x, *ff1, layer=l, final_norm=False)
        q, kt, v, fab, gate = _pre(x, row(g_mix[l]), w_in, w_gate, cos, sin, dft, layer=l,
                                   n_prompt=n_prompt, s_prompt=sp, s_sample=ss)
        lamv = jnp.stack([lam_q1[l], lam_k1[l], lam_q2[l], lam_k2[l]])
        o = _attn(q, kt, v, lamv, row(g_sub[l]), None, row0=0, batch=bp, seq=sp, lam_init=lam_init)
        o = _attn(q, kt, v, lamv, row(g_sub[l]), o, row0=n_prompt, batch=bs, seq=ss, lam_init=lam_init)
        f = _seqdft(fab, *seq_tabs[sp], None, row0=0, batch=bp, seq=sp)
        f = _seqdft(fab, *seq_tabs[ss], f, row0=n_prompt, batch=bs, seq=ss)
        x = _post(x, o, f, gate, w_pa, w_pf, w_o, layer=l)
        ff2 = (row(g_ff2[l]), w_ff2_up, w_ff2_down, gf)
        if l < DEPTH - 1:
            x = _ffn(x, *ff2, layer=l, final_norm=False)
    last = DEPTH - 1
    y_prompt = _ffn(x, *ff2, layer=last, final_norm=True, rows=n_prompt)
    y_sample = _ffn(x, *ff2, layer=last, final_norm=True, rows=n_sample, in_row0=n_prompt)
    return y_prompt.reshape(bp, sp, D_MODEL), y_sample.reshape(bs, ss, D_MODEL)
```

```python
import functools
import math

import jax
import jax.numpy as jnp
from jax import lax
from jax.experimental import pallas as pl
from jax.experimental.pallas import tpu as pltpu

D_MODEL = 1024
DEPTH = 4
N_HEADS = 4
HEAD_DIM = 64
V_DIM = 2 * HEAD_DIM
QK_W = N_HEADS * 2 * HEAD_DIM
ATTN_W = N_HEADS * V_DIM
N_FGROUPS = 4
FGROUP_DIM = 128
FOURIER_W = N_FGROUPS * FGROUP_DIM
IN_W = 2 * QK_W + ATTN_W + FOURIER_W
D_FF = 2816
ROPE_THETA = 10000.0
EPS = 1e-6

LANES = 128
QKV_W = 2 * QK_W + ATTN_W
Q_SCALE = HEAD_DIM ** -0.5 * math.log2(math.e)
VMEM_LIMIT = 56 * 1024 * 1024
TOKEN_TILE = 512
ATTN_TILE = 256 * 4096
FFT_RADIX = 16
F32_ROWS = 8

F32 = jnp.float32
BF16 = jnp.bfloat16


def _rms(x, g):
    return x * lax.rsqrt(jnp.mean(x * x, axis=-1, keepdims=True) + EPS) * g


def _dot(a, b):
    return jnp.dot(a, b, preferred_element_type=F32)


def _params(sem):
    return pltpu.CompilerParams(dimension_semantics=sem, vmem_limit_bytes=VMEM_LIMIT)


def _const_spec(shape):
    return pl.BlockSpec(shape, lambda *_: (0,) * len(shape), pipeline_mode=pl.Buffered(1))


def _layer_spec(shape, layer, col_block=0):
    return pl.BlockSpec((None,) + shape, lambda *_: (layer, 0, col_block), pipeline_mode=pl.Buffered(1))


def _ffn_kernel(x_ref, g_ref, wg_ref, wu_ref, wd_ref, gf_ref, *rest, final_norm):
    o_ref = rest[-1]
    x = x_ref[...]
    h = _rms(x, g_ref[...]).astype(BF16)
    gate = _dot(h, wg_ref[...])
    up = _dot(h, wu_ref[...])
    act = (gate / (1.0 + jnp.exp(-gate)) * up).astype(BF16)
    y = x + 0.5 * _dot(act, wd_ref[...])
    if final_norm:
        y = _rms(y, gf_ref[...])
    o_ref[...] = y


def _ffn(x, g, w_up, w_down, g_final, *, layer, final_norm, rows=None, in_row0=0, out_rows=None, out_row0=0,
         out_prev=None):
    tm = TOKEN_TILE
    rows = x.shape[0] if rows is None else rows
    out_rows = rows if out_rows is None else out_rows
    in0, out0 = in_row0 // tm, out_row0 // tm
    in_specs = [
        pl.BlockSpec((tm, D_MODEL), lambda i: (in0 + i, 0)),
        _const_spec((1, D_MODEL)),
        _layer_spec((D_MODEL, D_FF), layer, 0),
        _layer_spec((D_MODEL, D_FF), layer, 1),
        _layer_spec((D_FF, D_MODEL), layer),
        _const_spec((1, D_MODEL)),
    ]
    args = [x, g, w_up, w_up, w_down, g_final]
    aliases = {}
    if out_prev is not None:
        in_specs.append(pl.BlockSpec(memory_space=pl.ANY))
        args.append(out_prev)
        aliases = {len(args) - 1: 0}
    return pl.pallas_call(
        functools.partial(_ffn_kernel, final_norm=final_norm),
        grid=(rows // tm,),
        in_specs=in_specs,
        out_specs=pl.BlockSpec((tm, D_MODEL), lambda i: (out0 + i, 0)),
        out_shape=jax.ShapeDtypeStruct((out_rows, D_MODEL), F32),
        input_output_aliases=aliases,
        compiler_params=_params(("parallel",)),
        name="ffn",
    )(*args)


def _pre_kernel(x_ref, g_ref, win_ref, wgate_ref, cos_ref, sin_ref, dft_ref,
                q_ref, kt_ref, v_ref, fab_ref, gate_ref):
    h = _rms(x_ref[...], g_ref[...]).astype(BF16)
    proj = _dot(h, win_ref[...])
    cos = cos_ref[...]
    sin = sin_ref[...]
    lane = lax.broadcasted_iota(jnp.int32, cos.shape, 1)
    low_half = (lane % HEAD_DIM) < (HEAD_DIM // 2)
    for j in range(2 * N_HEADS):
        t = proj[:, j * LANES:(j + 1) * LANES]
        rot = jnp.where(low_half,
                        pltpu.roll(t, LANES - HEAD_DIM // 2, axis=1),
                        pltpu.roll(t, HEAD_DIM // 2, axis=1))
        roped = t * cos + rot * sin
        if j < N_HEADS:
            q_ref[:, j * LANES:(j + 1) * LANES] = (roped * Q_SCALE).astype(BF16)
        else:
            kt_ref[(j - N_HEADS) * LANES:(j - N_HEADS + 1) * LANES, :] = roped.T.astype(BF16)
    v_ref[...] = proj[:, 2 * QK_W:QKV_W].astype(BF16)
    dft = dft_ref[...]
    for grp in range(N_FGROUPS):
        u = proj[:, QKV_W + grp * FGROUP_DIM:QKV_W + (grp + 1) * FGROUP_DIM].astype(BF16)
        ab = _dot(u, dft)
        fab_ref[:, grp * FGROUP_DIM:(grp + 1) * FGROUP_DIM] = ab[:, :FGROUP_DIM].astype(BF16)
        fab_ref[:, FOURIER_W + grp * FGROUP_DIM:FOURIER_W + (grp + 1) * FGROUP_DIM] = (
            ab[:, FGROUP_DIM:].astype(BF16))
    z = _dot(h, wgate_ref[...])
    gate_ref[...] = (1.0 / (1.0 + jnp.exp(-z))).astype(BF16)


def _pre(x, g, w_in, w_gate, cos, sin, dft, *, layer, n_prompt, s_prompt, s_sample):
    t = x.shape[0]
    tm = TOKEN_TILE
    n_prompt_tiles = n_prompt // tm

    def pos_map(i):
        p = jnp.where(i < n_prompt_tiles, i % (s_prompt // tm), (i - n_prompt_tiles) % (s_sample // tm))
        return (p, 0)

    return pl.pallas_call(
        _pre_kernel,
        grid=(t // tm,),
        in_specs=[
            pl.BlockSpec((tm, D_MODEL), lambda i: (i, 0)),
            _const_spec((1, D_MODEL)),
            _layer_spec((D_MODEL, IN_W), layer),
            _layer_spec((D_MODEL, 2 * D_MODEL), layer),
            pl.BlockSpec((tm, LANES), pos_map),
            pl.BlockSpec((tm, LANES), pos_map),
            _const_spec((FGROUP_DIM, 2 * FGROUP_DIM)),
        ],
        out_specs=[
            pl.BlockSpec((tm, QK_W), lambda i: (i, 0)),
            pl.BlockSpec((QK_W, tm), lambda i: (0, i)),
            pl.BlockSpec((tm, ATTN_W), lambda i: (i, 0)),
            pl.BlockSpec((tm, 2 * FOURIER_W), lambda i: (i, 0)),
            pl.BlockSpec((tm, 2 * D_MODEL), lambda i: (i, 0)),
        ],
        out_shape=[
            jax.ShapeDtypeStruct((t, QK_W), BF16),
            jax.ShapeDtypeStruct((QK_W, t), BF16),
            jax.ShapeDtypeStruct((t, ATTN_W), BF16),
            jax.ShapeDtypeStruct((t, 2 * FOURIER_W), BF16),
            jax.ShapeDtypeStruct((t, 2 * D_MODEL), BF16),
        ],
        compiler_params=_params(("parallel",)),
        name="mixer_in",
    )(x, g, w_in, w_gate, cos, sin, dft)


def _attn_kernel(lamv_ref, q_ref, kt_ref, v_ref, gsub_ref, *rest, lam_init, n_items):
    o_ref = rest[-9]
    slots = ((rest[-8:-6], rest[-4:-2]), (rest[-6:-4], rest[-2:]))
    t = pl.program_id(0)

    def scores(slot):
        s_refs, m_refs = slot
        q = q_ref[...]
        kt = kt_ref[...]
        lane = lax.broadcasted_iota(jnp.int32, q.shape, 1)
        zero = jnp.zeros_like(q)
        for comp, qc in enumerate((jnp.where(lane < HEAD_DIM, q, zero), jnp.where(lane >= HEAD_DIM, q, zero))):
            s = _dot(qc, kt)
            s_refs[comp][...] = s
            m_refs[comp][...] = jnp.max(s, axis=-1, keepdims=True)

    def finish(slot):
        s_refs, m_refs = slot
        lamv = lamv_ref[...]
        lam = (jnp.exp(jnp.sum(lamv[0:1] * lamv[1:2], axis=-1, keepdims=True))
               - jnp.exp(jnp.sum(lamv[2:3] * lamv[3:4], axis=-1, keepdims=True)) + lam_init)
        v = v_ref[...]
        v1 = jnp.concatenate([v, jnp.ones_like(v)], axis=1)
        r = [_dot(jnp.exp2(s_refs[comp][...] - m_refs[comp][...]).astype(BF16), v1) for comp in range(2)]
        o = r[0][:, :V_DIM] / r[0][:, V_DIM:V_DIM + 1] - lam * (r[1][:, :V_DIM] / r[1][:, V_DIM:V_DIM + 1])
        o_ref[...] = (_rms(o, gsub_ref[...]) * (1.0 - lam_init)).astype(BF16)

    middle = jnp.logical_and(t > 0, t < n_items)

    @pl.when(t == 0)
    def _():
        scores(slots[0])

    for parity in range(2):
        @pl.when(jnp.logical_and(middle, t % 2 == parity))
        def _():
            scores(slots[parity])
            finish(slots[1 - parity])

    @pl.when(t == n_items)
    def _():
        finish(slots[(n_items - 1) % 2])


def _attn(q, kt, v, lamv, gsub, o_prev, *, row0, batch, seq, lam_init):
    t = q.shape[0]
    tq = ATTN_TILE // seq
    b0 = row0 // seq
    q0 = row0 // tq
    nq = seq // tq
    n_items = batch * N_HEADS * nq

    def item(step):
        return step // (N_HEADS * nq), (step // nq) % N_HEADS, step % nq

    def scored(step):
        return item(jnp.minimum(step, n_items - 1))

    def finished(step):
        return item(jnp.maximum(step - 1, 0))

    def q_map(step):
        b, h, i = scored(step)
        return (q0 + b * nq + i, h)

    def kt_map(step):
        b, h, _ = scored(step)
        return (h, b0 + b)

    def v_map(step):
        b, h, _ = finished(step)
        return (b0 + b, h)

    def o_map(step):
        b, h, i = finished(step)
        return (q0 + b * nq + i, h)

    in_specs = [
        _const_spec((4, HEAD_DIM)),
        pl.BlockSpec((tq, LANES), q_map),
        pl.BlockSpec((LANES, seq), kt_map),
        pl.BlockSpec((seq, V_DIM), v_map),
        _const_spec((1, V_DIM)),
    ]
    args = [lamv, q, kt, v, gsub]
    aliases = {}
    if o_prev is not None:
        in_specs.append(pl.BlockSpec(memory_space=pl.ANY))
        args.append(o_prev)
        aliases = {len(args) - 1: 0}
    return pl.pallas_call(
        functools.partial(_attn_kernel, lam_init=lam_init, n_items=n_items),
        grid=(n_items + 1,),
        in_specs=in_specs,
        out_specs=pl.BlockSpec((tq, V_DIM), o_map),
        out_shape=jax.ShapeDtypeStruct((t, ATTN_W), BF16),
        scratch_shapes=[pltpu.VMEM((tq, seq), F32)] * 4 + [pltpu.VMEM((tq, 1), F32)] * 4,
        input_output_aliases=aliases,
        compiler_params=_params(("arbitrary",)),
        name=f"diff_attn_s{seq}",
    )(*args)


def _cmul_const(x, w):
    re, im = x
    wr, wi = round(w.real, 12), round(w.imag, 12)
    if (wr, wi) == (1.0, 0.0):
        return re, im
    if (wr, wi) == (-1.0, 0.0):
        return -re, -im
    if (wr, wi) == (0.0, -1.0):
        return im, -re
    if (wr, wi) == (0.0, 1.0):
        return -im, re
    return re * wr - im * wi, re * wi + im * wr


def _fft(xs):
    n = len(xs)
    if n == 1:
        return xs
    even, odd = _fft(xs[0::2]), _fft(xs[1::2])
    out = [None] * n
    for k in range(n // 2):
        tr, ti = _cmul_const(odd[k], complex(math.cos(2 * math.pi * k / n), -math.sin(2 * math.pi * k / n)))
        er, ei = even[k]
        out[k] = (er + tr, ei + ti)
        out[k + n // 2] = (er - tr, ei - ti)
    return out


def _seqfft_kernel(m1_ref, twc_ref, tws_ref, fab_ref, *rest, scale):
    o_ref, wre, wim = rest[-3:]
    n = m1_ref.shape[1]
    for n2 in range(FFT_RADIX):
        ab = fab_ref[:, n2 * 2 * FOURIER_W:(n2 + 1) * 2 * FOURIER_W]
        p = _dot(m1_ref[...], ab)
        t_re = p[:n, :FOURIER_W] - p[n:, FOURIER_W:]
        t_im = -(p[:n, FOURIER_W:] + p[n:, :FOURIER_W])
        c = jnp.concatenate([twc_ref[n2]] * (FOURIER_W // LANES), axis=1)
        s = jnp.concatenate([tws_ref[n2]] * (FOURIER_W // LANES), axis=1)
        wre[n2] = t_re * c + t_im * s
        wim[n2] = t_im * c - t_re * s
    for g in range(n // F32_ROWS):
        rows = slice(g * F32_ROWS, (g + 1) * F32_ROWS)
        for lb in range(FOURIER_W // LANES):
            cols = slice(lb * LANES, (lb + 1) * LANES)
            ys = _fft([(wre[n2, rows, cols], wim[n2, rows, cols]) for n2 in range(FFT_RADIX)])
            for k2 in range(FFT_RADIX):
                o_ref[k2 * n + g * F32_ROWS:k2 * n + (g + 1) * F32_ROWS, cols] = (ys[k2][0] * scale).astype(BF16)


def _fft_tables(seq):
    n = seq // FFT_RADIX
    cr, sr = _dft_tables(n, n, n)
    twc, tws = _dft_tables(FFT_RADIX, n, seq)
    lanes = lambda a: jnp.broadcast_to(a[:, :, None], (FFT_RADIX, n, LANES))
    return jnp.concatenate([cr, sr], axis=0).astype(BF16), lanes(twc), lanes(tws)


def _seqfft(fab, m1, twc, tws, f_prev, *, row0, batch, seq):
    t = fab.shape[0]
    n = seq // FFT_RADIX
    b0 = row0 // seq
    fab_r = fab.reshape(t // FFT_RADIX, FFT_RADIX * 2 * FOURIER_W)
    in_specs = [
        _const_spec((2 * n, n)),
        _const_spec((FFT_RADIX, n, LANES)),
        _const_spec((FFT_RADIX, n, LANES)),
        pl.BlockSpec((n, FFT_RADIX * 2 * FOURIER_W), lambda b: (b0 + b, 0)),
    ]
    args = [m1, twc, tws, fab_r]
    aliases = {}
    if f_prev is not None:
        in_specs.append(pl.BlockSpec(memory_space=pl.ANY))
        args.append(f_prev)
        aliases = {len(args) - 1: 0}
    return pl.pallas_call(
        functools.partial(_seqfft_kernel, scale=1.0 / math.sqrt(seq * FGROUP_DIM)),
        grid=(batch,),
        in_specs=in_specs,
        out_specs=pl.BlockSpec((seq, FOURIER_W), lambda b: (b0 + b, 0)),
        out_shape=jax.ShapeDtypeStruct((t, FOURIER_W), BF16),
        scratch_shapes=[pltpu.VMEM((FFT_RADIX, n, FOURIER_W), F32)] * 2,
        input_output_aliases=aliases,
        compiler_params=_params(("parallel",)),
        name=f"seq_fft_s{seq}",
    )(*args)


def _post_kernel(x_ref, o_ref, f_ref, gate_ref, wpa_ref, wpf_ref, wo_ref, out_ref):
    br_a = _dot(o_ref[...], wpa_ref[...])
    br_f = _dot(f_ref[...], wpf_ref[...])
    merged = (gate_ref[:, :D_MODEL].astype(F32) * br_a + gate_ref[:, D_MODEL:].astype(F32) * br_f)
    out_ref[...] = x_ref[...] + _dot(merged.astype(BF16), wo_ref[...])


def _post(x, o, f, gate, w_pa, w_pf, w_o, *, layer):
    t = x.shape[0]
    tm = TOKEN_TILE
    return pl.pallas_call(
        _post_kernel,
        grid=(t // tm,),
        in_specs=[
            pl.BlockSpec((tm, D_MODEL), lambda i: (i, 0)),
            pl.BlockSpec((tm, ATTN_W), lambda i: (i, 0)),
            pl.BlockSpec((tm, FOURIER_W), lambda i: (i, 0)),
            pl.BlockSpec((tm, 2 * D_MODEL), lambda i: (i, 0)),
            _layer_spec((ATTN_W, D_MODEL), layer),
            _layer_spec((FOURIER_W, D_MODEL), layer),
            _layer_spec((D_MODEL, D_MODEL), layer),
        ],
        out_specs=pl.BlockSpec((tm, D_MODEL), lambda i: (i, 0)),
        out_shape=jax.ShapeDtypeStruct((t, D_MODEL), F32),
        compiler_params=_params(("parallel",)),
        name="mixer_out",
    )(x, o, f, gate, w_pa, w_pf, w_o)


def _rope_tables(seq):
    inv = 1.0 / (ROPE_THETA ** (jnp.arange(0, HEAD_DIM, 2, dtype=F32) / HEAD_DIM))
    ang = jnp.arange(seq, dtype=F32)[:, None] * inv[None, :]
    ang = jnp.concatenate([ang, ang, ang, ang], axis=-1)
    lane = jnp.arange(LANES)
    sign = jnp.where((lane % HEAD_DIM) < HEAD_DIM // 2, -1.0, 1.0).astype(F32)
    return jnp.cos(ang), jnp.sin(ang) * sign[None, :]


def _dft_tables(rows, cols, n):
    j = jnp.arange(rows, dtype=jnp.int32)
    k = jnp.arange(cols, dtype=jnp.int32)
    ang = ((j[:, None] * k[None, :]) % n).astype(F32) * (2.0 * math.pi / n)
    return jnp.cos(ang), jnp.sin(ang)


def kernel(x_prompt, x_sample, g_ff1, w_ff1_up, w_ff1_down, g_mix, w_in, lam_q1, lam_k1, lam_q2, lam_k2,
           g_sub, w_pa, w_pf, w_gate, w_o, g_ff2, w_ff2_up, w_ff2_down, g_final):
    bp, sp, _ = x_prompt.shape
    bs, ss, _ = x_sample.shape
    n_prompt = bp * sp
    assert n_prompt % ss == 0 and sp <= ss
    n_sample = bs * ss
    n_tok = n_prompt + n_sample

    cos, sin = _rope_tables(ss)
    cg, sg = _dft_tables(FGROUP_DIM, FGROUP_DIM, FGROUP_DIM)
    dft = jnp.concatenate([cg, sg], axis=1).astype(BF16)
    seq_tabs = {s: _fft_tables(s) for s in {sp, ss}}

    row = lambda v: v.reshape(1, -1)
    gf = row(g_final)
    w_ff1_up, w_ff1_down, w_ff2_up, w_ff2_down, w_in, w_gate, w_pa, w_pf, w_o = (
        w.astype(BF16) for w in (w_ff1_up, w_ff1_down, w_ff2_up, w_ff2_down, w_in, w_gate, w_pa, w_pf, w_o))
    x = None
    for l in range(DEPTH):
        lam_init = 0.8 - 0.6 * math.exp(-0.3 * l)
        ff1 = (row(g_ff1[l]), w_ff1_up, w_ff1_down, gf)
        if l == 0:
            x = _ffn(x_prompt.reshape(n_prompt, D_MODEL), *ff1, layer=l, final_norm=False, out_rows=n_tok)
            x = _ffn(x_sample.reshape(n_sample, D_MODEL), *ff1, layer=l, final_norm=False, out_rows=n_tok,
                     out_row0=n_prompt, out_prev=x)
        else:
            x = _ffn(x, *ff1, layer=l, final_norm=False)
        q, kt, v, fab, gate = _pre(x, row(g_mix[l]), w_in, w_gate, cos, sin, dft, layer=l,
                                   n_prompt=n_prompt, s_prompt=sp, s_sample=ss)
        lamv = jnp.stack([lam_q1[l], lam_k1[l], lam_q2[l], lam_k2[l]])
        o = _attn(q, kt, v, lamv, row(g_sub[l]), None, row0=0, batch=bp, seq=sp, lam_init=lam_init)
        o = _attn(q, kt, v, lamv, row(g_sub[l]), o, row0=n_prompt, batch=bs, seq=ss, lam_init=lam_init)
        f = _seqfft(fab, *seq_tabs[sp], None, row0=0, batch=bp, seq=sp)
        f = _seqfft(fab, *seq_tabs[ss], f, row0=n_prompt, batch=bs, seq=ss)
        x = _post(x, o, f, gate, w_pa, w_pf, w_o, layer=l)
        ff2 = (row(g_ff2[l]), w_ff2_up, w_ff2_down, gf)
        if l < DEPTH - 1:
            x = _ffn(x, *ff2, layer=l, final_norm=False)
    last = DEPTH - 1
    y_prompt = _ffn(x, *ff2, layer=last, final_norm=True, rows=n_prompt)
    y_sample = _ffn(x, *ff2, layer=last, final_norm=True, rows=n_sample, in_row0=n_prompt)
    return y_prompt.reshape(bp, sp, D_MODEL), y_sample.reshape(bs, ss, D_MODEL)
```

```python
import functools
import math

import jax
import jax.numpy as jnp
from jax import lax
from jax.experimental import pallas as pl
from jax.experimental.pallas import tpu as pltpu

D_MODEL = 1024
DEPTH = 4
N_HEADS = 4
HEAD_DIM = 64
V_DIM = 2 * HEAD_DIM
QK_W = N_HEADS * 2 * HEAD_DIM
ATTN_W = N_HEADS * V_DIM
N_FGROUPS = 4
FGROUP_DIM = 128
FOURIER_W = N_FGROUPS * FGROUP_DIM
IN_W = 2 * QK_W + ATTN_W + FOURIER_W
D_FF = 2816
ROPE_THETA = 10000.0
EPS = 1e-6

LANES = 128
QKV_W = 2 * QK_W + ATTN_W
Q_SCALE = HEAD_DIM ** -0.5 * math.log2(math.e)
VMEM_LIMIT = 56 * 1024 * 1024
TOKEN_TILE = 512
ATTN_TILE = 256 * 4096
FFT_RADIX = 16
F32_ROWS = 8

F32 = jnp.float32
BF16 = jnp.bfloat16


def _rms(x, g):
    return x * lax.rsqrt(jnp.mean(x * x, axis=-1, keepdims=True) + EPS) * g


def _dot(a, b):
    return jnp.dot(a, b, preferred_element_type=F32)


def _params(sem):
    return pltpu.CompilerParams(dimension_semantics=sem, vmem_limit_bytes=VMEM_LIMIT)


def _const_spec(shape):
    return pl.BlockSpec(shape, lambda *_: (0,) * len(shape), pipeline_mode=pl.Buffered(1))


def _layer_spec(shape, layer, col_block=0):
    return pl.BlockSpec((None,) + shape, lambda *_: (layer, 0, col_block), pipeline_mode=pl.Buffered(1))


def _ffn_kernel(x_ref, g_ref, wg_ref, wu_ref, wd_ref, gf_ref, *rest, final_norm):
    o_ref = rest[-1]
    x = x_ref[...]
    h = _rms(x, g_ref[...]).astype(BF16)
    gate = _dot(h, wg_ref[...])
    up = _dot(h, wu_ref[...])
    act = (gate / (1.0 + jnp.exp(-gate)) * up).astype(BF16)
    y = x + 0.5 * _dot(act, wd_ref[...])
    if final_norm:
        y = _rms(y, gf_ref[...])
    o_ref[...] = y


def _ffn(x, g, w_up, w_down, g_final, *, layer, final_norm, rows=None, in_row0=0, out_rows=None, out_row0=0,
         out_prev=None):
    tm = TOKEN_TILE
    rows = x.shape[0] if rows is None else rows
    out_rows = rows if out_rows is None else out_rows
    in0, out0 = in_row0 // tm, out_row0 // tm
    in_specs = [
        pl.BlockSpec((tm, D_MODEL), lambda i: (in0 + i, 0)),
        _const_spec((1, D_MODEL)),
        _layer_spec((D_MODEL, D_FF), layer, 0),
        _layer_spec((D_MODEL, D_FF), layer, 1),
        _layer_spec((D_FF, D_MODEL), layer),
        _const_spec((1, D_MODEL)),
    ]
    args = [x, g, w_up, w_up, w_down, g_final]
    aliases = {}
    if out_prev is not None:
        in_specs.append(pl.BlockSpec(memory_space=pl.ANY))
        args.append(out_prev)
        aliases = {len(args) - 1: 0}
    return pl.pallas_call(
        functools.partial(_ffn_kernel, final_norm=final_norm),
        grid=(rows // tm,),
        in_specs=in_specs,
        out_specs=pl.BlockSpec((tm, D_MODEL), lambda i: (out0 + i, 0)),
        out_shape=jax.ShapeDtypeStruct((out_rows, D_MODEL), F32),
        input_output_aliases=aliases,
        compiler_params=_params(("parallel",)),
        name="ffn",
    )(*args)


def _pre_kernel(x_ref, g_ref, win_ref, wgate_ref, cos_ref, sin_ref, dft_ref,
                q_ref, kt_ref, v_ref, fab_ref, gate_ref, ab_scr):
    h = _rms(x_ref[...], g_ref[...]).astype(BF16)
    proj = _dot(h, win_ref[...])
    cos = cos_ref[...]
    sin = sin_ref[...]
    lane = lax.broadcasted_iota(jnp.int32, cos.shape, 1)
    low_half = (lane % HEAD_DIM) < (HEAD_DIM // 2)
    for j in range(2 * N_HEADS):
        t = proj[:, j * LANES:(j + 1) * LANES]
        rot = jnp.where(low_half,
                        pltpu.roll(t, LANES - HEAD_DIM // 2, axis=1),
                        pltpu.roll(t, HEAD_DIM // 2, axis=1))
        roped = t * cos + rot * sin
        if j < N_HEADS:
            q_ref[:, j * LANES:(j + 1) * LANES] = (roped * Q_SCALE).astype(BF16)
        else:
            kt_ref[(j - N_HEADS) * LANES:(j - N_HEADS + 1) * LANES, :] = roped.T.astype(BF16)
    v_ref[...] = proj[:, 2 * QK_W:QKV_W].astype(BF16)
    dft = dft_ref[...]
    for grp in range(N_FGROUPS):
        u = proj[:, QKV_W + grp * FGROUP_DIM:QKV_W + (grp + 1) * FGROUP_DIM].astype(BF16)
        ab = _dot(u, dft)
        ab_scr[grp] = ab[:, :FGROUP_DIM]
        ab_scr[N_FGROUPS + grp] = ab[:, FGROUP_DIM:]
    rows = ab_scr.shape[1] // FFT_RADIX
    for n2 in range(FFT_RADIX):
        for j in range(2 * N_FGROUPS):
            fab_ref[:, n2 * 2 * FOURIER_W + j * LANES:n2 * 2 * FOURIER_W + (j + 1) * LANES] = (
                ab_scr[j, pl.ds(n2, rows, stride=FFT_RADIX), :].astype(BF16))
    z = _dot(h, wgate_ref[...])
    gate_ref[...] = (1.0 / (1.0 + jnp.exp(-z))).astype(BF16)


def _pre(x, g, w_in, w_gate, cos, sin, dft, *, layer, n_prompt, s_prompt, s_sample):
    t = x.shape[0]
    tm = TOKEN_TILE
    n_prompt_tiles = n_prompt // tm

    def pos_map(i):
        p = jnp.where(i < n_prompt_tiles, i % (s_prompt // tm), (i - n_prompt_tiles) % (s_sample // tm))
        return (p, 0)

    return pl.pallas_call(
        _pre_kernel,
        grid=(t // tm,),
        in_specs=[
            pl.BlockSpec((tm, D_MODEL), lambda i: (i, 0)),
            _const_spec((1, D_MODEL)),
            _layer_spec((D_MODEL, IN_W), layer),
            _layer_spec((D_MODEL, 2 * D_MODEL), layer),
            pl.BlockSpec((tm, LANES), pos_map),
            pl.BlockSpec((tm, LANES), pos_map),
            _const_spec((FGROUP_DIM, 2 * FGROUP_DIM)),
        ],
        out_specs=[
            pl.BlockSpec((tm, QK_W), lambda i: (i, 0)),
            pl.BlockSpec((QK_W, tm), lambda i: (0, i)),
            pl.BlockSpec((tm, ATTN_W), lambda i: (i, 0)),
            pl.BlockSpec((tm // FFT_RADIX, FFT_RADIX * 2 * FOURIER_W), lambda i: (i, 0)),
            pl.BlockSpec((tm, 2 * D_MODEL), lambda i: (i, 0)),
        ],
        out_shape=[
            jax.ShapeDtypeStruct((t, QK_W), BF16),
            jax.ShapeDtypeStruct((QK_W, t), BF16),
            jax.ShapeDtypeStruct((t, ATTN_W), BF16),
            jax.ShapeDtypeStruct((t // FFT_RADIX, FFT_RADIX * 2 * FOURIER_W), BF16),
            jax.ShapeDtypeStruct((t, 2 * D_MODEL), BF16),
        ],
        scratch_shapes=[pltpu.VMEM((2 * N_FGROUPS, tm, LANES), F32)],
        compiler_params=_params(("parallel",)),
        name="mixer_in",
    )(x, g, w_in, w_gate, cos, sin, dft)


def _attn_kernel(lamv_ref, q_ref, kt_ref, v_ref, gsub_ref, *rest, lam_init, n_items):
    o_ref = rest[-9]
    slots = ((rest[-8:-6], rest[-4:-2]), (rest[-6:-4], rest[-2:]))
    t = pl.program_id(0)

    def scores(slot):
        s_refs, m_refs = slot
        q = q_ref[...]
        kt = kt_ref[...]
        lane = lax.broadcasted_iota(jnp.int32, q.shape, 1)
        zero = jnp.zeros_like(q)
        for comp, qc in enumerate((jnp.where(lane < HEAD_DIM, q, zero), jnp.where(lane >= HEAD_DIM, q, zero))):
            s = _dot(qc, kt)
            s_refs[comp][...] = s
            m_refs[comp][...] = jnp.max(s, axis=-1, keepdims=True)

    def finish(slot):
        s_refs, m_refs = slot
        lamv = lamv_ref[...]
        lam = (jnp.exp(jnp.sum(lamv[0:1] * lamv[1:2], axis=-1, keepdims=True))
               - jnp.exp(jnp.sum(lamv[2:3] * lamv[3:4], axis=-1, keepdims=True)) + lam_init)
        v = v_ref[...]
        v1 = jnp.concatenate([v, jnp.ones_like(v)], axis=1)
        r = [_dot(jnp.exp2(s_refs[comp][...] - m_refs[comp][...]).astype(BF16), v1) for comp in range(2)]
        o = r[0][:, :V_DIM] / r[0][:, V_DIM:V_DIM + 1] - lam * (r[1][:, :V_DIM] / r[1][:, V_DIM:V_DIM + 1])
        o_ref[...] = (_rms(o, gsub_ref[...]) * (1.0 - lam_init)).astype(BF16)

    middle = jnp.logical_and(t > 0, t < n_items)

    @pl.when(t == 0)
    def _():
        scores(slots[0])

    for parity in range(2):
        @pl.when(jnp.logical_and(middle, t % 2 == parity))
        def _():
            scores(slots[parity])
            finish(slots[1 - parity])

    @pl.when(t == n_items)
    def _():
        finish(slots[(n_items - 1) % 2])


def _attn(q, kt, v, lamv, gsub, o_prev, *, row0, batch, seq, lam_init):
    t = q.shape[0]
    tq = ATTN_TILE // seq
    b0 = row0 // seq
    q0 = row0 // tq
    nq = seq // tq
    n_items = batch * N_HEADS * nq

    def item(step):
        return step // (N_HEADS * nq), (step // nq) % N_HEADS, step % nq

    def scored(step):
        return item(jnp.minimum(step, n_items - 1))

    def finished(step):
        return item(jnp.maximum(step - 1, 0))

    def q_map(step):
        b, h, i = scored(step)
        return (q0 + b * nq + i, h)

    def kt_map(step):
        b, h, _ = scored(step)
        return (h, b0 + b)

    def v_map(step):
        b, h, _ = finished(step)
        return (b0 + b, h)

    def o_map(step):
        b, h, i = finished(step)
        return (q0 + b * nq + i, h)

    in_specs = [
        _const_spec((4, HEAD_DIM)),
        pl.BlockSpec((tq, LANES), q_map),
        pl.BlockSpec((LANES, seq), kt_map),
        pl.BlockSpec((seq, V_DIM), v_map),
        _const_spec((1, V_DIM)),
    ]
    args = [lamv, q, kt, v, gsub]
    aliases = {}
    if o_prev is not None:
        in_specs.append(pl.BlockSpec(memory_space=pl.ANY))
        args.append(o_prev)
        aliases = {len(args) - 1: 0}
    return pl.pallas_call(
        functools.partial(_attn_kernel, lam_init=lam_init, n_items=n_items),
        grid=(n_items + 1,),
        in_specs=in_specs,
        out_specs=pl.BlockSpec((tq, V_DIM), o_map),
        out_shape=jax.ShapeDtypeStruct((t, ATTN_W), BF16),
        scratch_shapes=[pltpu.VMEM((tq, seq), F32)] * 4 + [pltpu.VMEM((tq, 1), F32)] * 4,
        input_output_aliases=aliases,
        compiler_params=_params(("arbitrary",)),
        name=f"diff_attn_s{seq}",
    )(*args)


def _cmul_const(x, w):
    re, im = x
    wr, wi = round(w.real, 12), round(w.imag, 12)
    if (wr, wi) == (1.0, 0.0):
        return re, im
    if (wr, wi) == (-1.0, 0.0):
        return -re, -im
    if (wr, wi) == (0.0, -1.0):
        return im, -re
    if (wr, wi) == (0.0, 1.0):
        return -im, re
    return re * wr - im * wi, re * wi + im * wr


def _fft(xs):
    n = len(xs)
    if n == 1:
        return xs
    even, odd = _fft(xs[0::2]), _fft(xs[1::2])
    out = [None] * n
    for k in range(n // 2):
        tr, ti = _cmul_const(odd[k], complex(math.cos(2 * math.pi * k / n), -math.sin(2 * math.pi * k / n)))
        er, ei = even[k]
        out[k] = (er + tr, ei + ti)
        out[k + n // 2] = (er - tr, ei - ti)
    return out


def _seqfft_kernel(m1_ref, twc_ref, tws_ref, fab_ref, *rest, scale):
    o_ref, wre, wim = rest[-3:]
    n = m1_ref.shape[1]
    for n2 in range(FFT_RADIX):
        ab = fab_ref[:, n2 * 2 * FOURIER_W:(n2 + 1) * 2 * FOURIER_W]
        p = _dot(m1_ref[...], ab)
        t_re = p[:n, :FOURIER_W] - p[n:, FOURIER_W:]
        t_im = -(p[:n, FOURIER_W:] + p[n:, :FOURIER_W])
        c = jnp.concatenate([twc_ref[n2]] * (FOURIER_W // LANES), axis=1)
        s = jnp.concatenate([tws_ref[n2]] * (FOURIER_W // LANES), axis=1)
        wre[n2] = t_re * c + t_im * s
        wim[n2] = t_im * c - t_re * s
    for g in range(n // F32_ROWS):
        rows = slice(g * F32_ROWS, (g + 1) * F32_ROWS)
        for lb in range(FOURIER_W // LANES):
            cols = slice(lb * LANES, (lb + 1) * LANES)
            ys = _fft([(wre[n2, rows, cols], wim[n2, rows, cols]) for n2 in range(FFT_RADIX)])
            for k2 in range(FFT_RADIX):
                o_ref[k2 * n + g * F32_ROWS:k2 * n + (g + 1) * F32_ROWS, cols] = (ys[k2][0] * scale).astype(BF16)


def _fft_tables(seq):
    n = seq // FFT_RADIX
    cr, sr = _dft_tables(n, n, n)
    twc, tws = _dft_tables(FFT_RADIX, n, seq)
    lanes = lambda a: jnp.broadcast_to(a[:, :, None], (FFT_RADIX, n, LANES))
    return jnp.concatenate([cr, sr], axis=0).astype(BF16), lanes(twc), lanes(tws)


def _seqfft(fab, m1, twc, tws, f_prev, *, row0, batch, seq):
    t = fab.shape[0] * FFT_RADIX
    n = seq // FFT_RADIX
    b0 = row0 // seq
    in_specs = [
        _const_spec((2 * n, n)),
        _const_spec((FFT_RADIX, n, LANES)),
        _const_spec((FFT_RADIX, n, LANES)),
        pl.BlockSpec((n, FFT_RADIX * 2 * FOURIER_W), lambda b: (b0 + b, 0)),
    ]
    args = [m1, twc, tws, fab]
    aliases = {}
    if f_prev is not None:
        in_specs.append(pl.BlockSpec(memory_space=pl.ANY))
        args.append(f_prev)
        aliases = {len(args) - 1: 0}
    return pl.pallas_call(
        functools.partial(_seqfft_kernel, scale=1.0 / math.sqrt(seq * FGROUP_DIM)),
        grid=(batch,),
        in_specs=in_specs,
        out_specs=pl.BlockSpec((seq, FOURIER_W), lambda b: (b0 + b, 0)),
        out_shape=jax.ShapeDtypeStruct((t, FOURIER_W), BF16),
        scratch_shapes=[pltpu.VMEM((FFT_RADIX, n, FOURIER_W), F32)] * 2,
        input_output_aliases=aliases,
        compiler_params=_params(("parallel",)),
        name=f"seq_fft_s{seq}",
    )(*args)


def _post_kernel(x_ref, o_ref, f_ref, gate_ref, wpa_ref, wpf_ref, wo_ref, out_ref):
    br_a = _dot(o_ref[...], wpa_ref[...])
    br_f = _dot(f_ref[...], wpf_ref[...])
    merged = (gate_ref[:, :D_MODEL].astype(F32) * br_a + gate_ref[:, D_MODEL:].astype(F32) * br_f)
    out_ref[...] = x_ref[...] + _dot(merged.astype(BF16), wo_ref[...])


def _post(x, o, f, gate, w_pa, w_pf, w_o, *, layer):
    t = x.shape[0]
    tm = TOKEN_TILE
    return pl.pallas_call(
        _post_kernel,
        grid=(t // tm,),
        in_specs=[
            pl.BlockSpec((tm, D_MODEL), lambda i: (i, 0)),
            pl.BlockSpec((tm, ATTN_W), lambda i: (i, 0)),
            pl.BlockSpec((tm, FOURIER_W), lambda i: (i, 0)),
            pl.BlockSpec((tm, 2 * D_MODEL), lambda i: (i, 0)),
            _layer_spec((ATTN_W, D_MODEL), layer),
            _layer_spec((FOURIER_W, D_MODEL), layer),
            _layer_spec((D_MODEL, D_MODEL), layer),
        ],
        out_specs=pl.BlockSpec((tm, D_MODEL), lambda i: (i, 0)),
        out_shape=jax.ShapeDtypeStruct((t, D_MODEL), F32),
        compiler_params=_params(("parallel",)),
        name="mixer_out",
    )(x, o, f, gate, w_pa, w_pf, w_o)


def _rope_tables(seq):
    inv = 1.0 / (ROPE_THETA ** (jnp.arange(0, HEAD_DIM, 2, dtype=F32) / HEAD_DIM))
    ang = jnp.arange(seq, dtype=F32)[:, None] * inv[None, :]
    ang = jnp.concatenate([ang, ang, ang, ang], axis=-1)
    lane = jnp.arange(LANES)
    sign = jnp.where((lane % HEAD_DIM) < HEAD_DIM // 2, -1.0, 1.0).astype(F32)
    return jnp.cos(ang), jnp.sin(ang) * sign[None, :]


def _dft_tables(rows, cols, n):
    j = jnp.arange(rows, dtype=jnp.int32)
    k = jnp.arange(cols, dtype=jnp.int32)
    ang = ((j[:, None] * k[None, :]) % n).astype(F32) * (2.0 * math.pi / n)
    return jnp.cos(ang), jnp.sin(ang)


def kernel(x_prompt, x_sample, g_ff1, w_ff1_up, w_ff1_down, g_mix, w_in, lam_q1, lam_k1, lam_q2, lam_k2,
           g_sub, w_pa, w_pf, w_gate, w_o, g_ff2, w_ff2_up, w_ff2_down, g_final):
    bp, sp, _ = x_prompt.shape
    bs, ss, _ = x_sample.shape
    n_prompt = bp * sp
    assert n_prompt % ss == 0 and sp <= ss
    n_sample = bs * ss
    n_tok = n_prompt + n_sample

    cos, sin = _rope_tables(ss)
    cg, sg = _dft_tables(FGROUP_DIM, FGROUP_DIM, FGROUP_DIM)
    dft = jnp.concatenate([cg, sg], axis=1).astype(BF16)
    seq_tabs = {s: _fft_tables(s) for s in {sp, ss}}

    row = lambda v: v.reshape(1, -1)
    gf = row(g_final)
    w_ff1_up, w_ff1_down, w_ff2_up, w_ff2_down, w_in, w_gate, w_pa, w_pf, w_o = (
        w.astype(BF16) for w in (w_ff1_up, w_ff1_down, w_ff2_up, w_ff2_down, w_in, w_gate, w_pa, w_pf, w_o))
    x = None
    for l in range(DEPTH):
        lam_init = 0.8 - 0.6 * math.exp(-0.3 * l)
        ff1 = (row(g_ff1[l]), w_ff1_up, w_ff1_down, gf)
        if l == 0:
            x = _ffn(x_prompt.reshape(n_prompt, D_MODEL), *ff1, layer=l, final_norm=False, out_rows=n_tok)
            x = _ffn(x_sample.reshape(n_sample, D_MODEL), *ff1, layer=l, final_norm=False, out_rows=n_tok,
                     out_row0=n_prompt, out_prev=x)
        else:
            x = _ffn(x, *ff1, layer=l, final_norm=False)
        q, kt, v, fab, gate = _pre(x, row(g_mix[l]), w_in, w_gate, cos, sin, dft, layer=l,
                                   n_prompt=n_prompt, s_prompt=sp, s_sample=ss)
        lamv = jnp.stack([lam_q1[l], lam_k1[l], lam_q2[l], lam_k2[l]])
        o = _attn(q, kt, v, lamv, row(g_sub[l]), None, row0=0, batch=bp, seq=sp, lam_init=lam_init)
        o = _attn(q, kt, v, lamv, row(g_sub[l]), o, row0=n_prompt, batch=bs, seq=ss, lam_init=lam_init)
        f = _seqfft(fab, *seq_tabs[sp], None, row0=0, batch=bp, seq=sp)
        f = _seqfft(fab, *seq_tabs[ss], f, row0=n_prompt, batch=bs, seq=ss)
        x = _post(x, o, f, gate, w_pa, w_pf, w_o, layer=l)
        ff2 = (row(g_ff2[l]), w_ff2_up, w_ff2_down, gf)
        if l < DEPTH - 1:
            x = _ffn(x, *ff2, layer=l, final_norm=False)
    last = DEPTH - 1
    y_prompt = _ffn(x, *ff2, layer=last, final_norm=True, rows=n_prompt)
    y_sample = _ffn(x, *ff2, layer=last, final_norm=True, rows=n_sample, in_row0=n_prompt)
    return y_prompt.reshape(bp, sp, D_MODEL), y_sample.reshape(bs, ss, D_MODEL)
```

```python
import functools
import math

import jax
import jax.numpy as jnp
from jax import lax
from jax.experimental import pallas as pl
from jax.experimental.pallas import tpu as pltpu

D_MODEL = 1024
DEPTH = 4
N_HEADS = 4
HEAD_DIM = 64
V_DIM = 2 * HEAD_DIM
QK_W = N_HEADS * 2 * HEAD_DIM
ATTN_W = N_HEADS * V_DIM
N_FGROUPS = 4
FGROUP_DIM = 128
FOURIER_W = N_FGROUPS * FGROUP_DIM
IN_W = 2 * QK_W + ATTN_W + FOURIER_W
D_FF = 2816
ROPE_THETA = 10000.0
EPS = 1e-6

LANES = 128
QKV_W = 2 * QK_W + ATTN_W
Q_SCALE = HEAD_DIM ** -0.5 * math.log2(math.e)
VMEM_LIMIT = 56 * 1024 * 1024
TOKEN_TILE = 512
ATTN_TILE = 256 * 4096
FFT_RADIX = 16
F32_ROWS = 8
PERM_ROWS = 256

F32 = jnp.float32
BF16 = jnp.bfloat16


def _rms(x, g):
    return x * lax.rsqrt(jnp.mean(x * x, axis=-1, keepdims=True) + EPS) * g


def _dot(a, b):
    return jnp.dot(a, b, preferred_element_type=F32)


def _params(sem):
    return pltpu.CompilerParams(dimension_semantics=sem, vmem_limit_bytes=VMEM_LIMIT)


def _const_spec(shape):
    return pl.BlockSpec(shape, lambda *_: (0,) * len(shape), pipeline_mode=pl.Buffered(1))


def _layer_spec(shape, layer, col_block=0):
    return pl.BlockSpec((None,) + shape, lambda *_: (layer, 0, col_block), pipeline_mode=pl.Buffered(1))


def _ffn_kernel(x_ref, g_ref, wg_ref, wu_ref, wd_ref, gf_ref, *rest, final_norm):
    o_ref = rest[-1]
    x = x_ref[...]
    h = _rms(x, g_ref[...]).astype(BF16)
    gate = _dot(h, wg_ref[...])
    up = _dot(h, wu_ref[...])
    act = (gate / (1.0 + jnp.exp(-gate)) * up).astype(BF16)
    y = x + 0.5 * _dot(act, wd_ref[...])
    if final_norm:
        y = _rms(y, gf_ref[...])
    o_ref[...] = y


def _ffn(x, g, w_up, w_down, g_final, *, layer, final_norm, rows=None, in_row0=0, out_rows=None, out_row0=0,
         out_prev=None):
    tm = TOKEN_TILE
    rows = x.shape[0] if rows is None else rows
    out_rows = rows if out_rows is None else out_rows
    in0, out0 = in_row0 // tm, out_row0 // tm
    in_specs = [
        pl.BlockSpec((tm, D_MODEL), lambda i: (in0 + i, 0)),
        _const_spec((1, D_MODEL)),
        _layer_spec((D_MODEL, D_FF), layer, 0),
        _layer_spec((D_MODEL, D_FF), layer, 1),
        _layer_spec((D_FF, D_MODEL), layer),
        _const_spec((1, D_MODEL)),
    ]
    args = [x, g, w_up, w_up, w_down, g_final]
    aliases = {}
    if out_prev is not None:
        in_specs.append(pl.BlockSpec(memory_space=pl.ANY))
        args.append(out_prev)
        aliases = {len(args) - 1: 0}
    return pl.pallas_call(
        functools.partial(_ffn_kernel, final_norm=final_norm),
        grid=(rows // tm,),
        in_specs=in_specs,
        out_specs=pl.BlockSpec((tm, D_MODEL), lambda i: (out0 + i, 0)),
        out_shape=jax.ShapeDtypeStruct((out_rows, D_MODEL), F32),
        input_output_aliases=aliases,
        compiler_params=_params(("parallel",)),
        name="ffn",
    )(*args)


def _pre_kernel(x_ref, g_ref, win_ref, wgate_ref, cos_ref, sin_ref, dft_ref, perm_ref,
                q_ref, kt_ref, v_ref, fab_ref, gate_ref):
    h = _rms(x_ref[...], g_ref[...]).astype(BF16)
    proj = _dot(h, win_ref[...])
    cos = cos_ref[...]
    sin = sin_ref[...]
    lane = lax.broadcasted_iota(jnp.int32, cos.shape, 1)
    low_half = (lane % HEAD_DIM) < (HEAD_DIM // 2)
    for j in range(2 * N_HEADS):
        t = proj[:, j * LANES:(j + 1) * LANES]
        rot = jnp.where(low_half,
                        pltpu.roll(t, LANES - HEAD_DIM // 2, axis=1),
                        pltpu.roll(t, HEAD_DIM // 2, axis=1))
        roped = t * cos + rot * sin
        if j < N_HEADS:
            q_ref[:, j * LANES:(j + 1) * LANES] = (roped * Q_SCALE).astype(BF16)
        else:
            kt_ref[(j - N_HEADS) * LANES:(j - N_HEADS + 1) * LANES, :] = roped.T.astype(BF16)
    v_ref[...] = proj[:, 2 * QK_W:QKV_W].astype(BF16)
    dft = dft_ref[...]
    u = proj[:, QKV_W:].astype(BF16)
    span = PERM_ROWS // FFT_RADIX
    for part in range(u.shape[0] // PERM_ROWS):
        up = _dot(perm_ref[...], u[part * PERM_ROWS:(part + 1) * PERM_ROWS, :]).astype(BF16)
        for grp in range(N_FGROUPS):
            ab = _dot(up[:, grp * FGROUP_DIM:(grp + 1) * FGROUP_DIM], dft)
            for n2 in range(FFT_RADIX):
                src = slice(n2 * span, (n2 + 1) * span)
                dst = slice(part * span, (part + 1) * span)
                col = n2 * 2 * FOURIER_W + grp * FGROUP_DIM
                fab_ref[dst, col:col + FGROUP_DIM] = ab[src, :FGROUP_DIM].astype(BF16)
                fab_ref[dst, col + FOURIER_W:col + FOURIER_W + FGROUP_DIM] = ab[src, FGROUP_DIM:].astype(BF16)
    z = _dot(h, wgate_ref[...])
    gate_ref[...] = (1.0 / (1.0 + jnp.exp(-z))).astype(BF16)


def _pre(x, g, w_in, w_gate, cos, sin, dft, *, layer, n_prompt, s_prompt, s_sample):
    t = x.shape[0]
    tm = TOKEN_TILE
    n_prompt_tiles = n_prompt // tm
    idx = jnp.arange(PERM_ROWS)
    src = FFT_RADIX * (idx % (PERM_ROWS // FFT_RADIX)) + idx // (PERM_ROWS // FFT_RADIX)
    perm = (src[:, None] == idx[None, :]).astype(BF16)

    def pos_map(i):
        p = jnp.where(i < n_prompt_tiles, i % (s_prompt // tm), (i - n_prompt_tiles) % (s_sample // tm))
        return (p, 0)

    return pl.pallas_call(
        _pre_kernel,
        grid=(t // tm,),
        in_specs=[
            pl.BlockSpec((tm, D_MODEL), lambda i: (i, 0)),
            _const_spec((1, D_MODEL)),
            _layer_spec((D_MODEL, IN_W), layer),
            _layer_spec((D_MODEL, 2 * D_MODEL), layer),
            pl.BlockSpec((tm, LANES), pos_map),
            pl.BlockSpec((tm, LANES), pos_map),
            _const_spec((FGROUP_DIM, 2 * FGROUP_DIM)),
            _const_spec((PERM_ROWS, PERM_ROWS)),
        ],
        out_specs=[
            pl.BlockSpec((tm, QK_W), lambda i: (i, 0)),
            pl.BlockSpec((QK_W, tm), lambda i: (0, i)),
            pl.BlockSpec((tm, ATTN_W), lambda i: (i, 0)),
            pl.BlockSpec((tm // FFT_RADIX, FFT_RADIX * 2 * FOURIER_W), lambda i: (i, 0)),
            pl.BlockSpec((tm, 2 * D_MODEL), lambda i: (i, 0)),
        ],
        out_shape=[
            jax.ShapeDtypeStruct((t, QK_W), BF16),
            jax.ShapeDtypeStruct((QK_W, t), BF16),
            jax.ShapeDtypeStruct((t, ATTN_W), BF16),
            jax.ShapeDtypeStruct((t // FFT_RADIX, FFT_RADIX * 2 * FOURIER_W), BF16),
            jax.ShapeDtypeStruct((t, 2 * D_MODEL), BF16),
        ],
        compiler_params=_params(("parallel",)),
        name="mixer_in",
    )(x, g, w_in, w_gate, cos, sin, dft, perm)


def _attn_kernel(lamv_ref, q_ref, kt_ref, v_ref, gsub_ref, *rest, lam_init, n_items):
    o_ref = rest[-9]
    slots = ((rest[-8:-6], rest[-4:-2]), (rest[-6:-4], rest[-2:]))
    t = pl.program_id(0)

    def scores(slot):
        s_refs, m_refs = slot
        q = q_ref[...]
        kt = kt_ref[...]
        lane = lax.broadcasted_iota(jnp.int32, q.shape, 1)
        zero = jnp.zeros_like(q)
        for comp, qc in enumerate((jnp.where(lane < HEAD_DIM, q, zero), jnp.where(lane >= HEAD_DIM, q, zero))):
            s = _dot(qc, kt)
            s_refs[comp][...] = s
            m_refs[comp][...] = jnp.max(s, axis=-1, keepdims=True)

    def finish(slot):
        s_refs, m_refs = slot
        lamv = lamv_ref[...]
        lam = (jnp.exp(jnp.sum(lamv[0:1] * lamv[1:2], axis=-1, keepdims=True))
               - jnp.exp(jnp.sum(lamv[2:3] * lamv[3:4], axis=-1, keepdims=True)) + lam_init)
        v = v_ref[...]
        v1 = jnp.concatenate([v, jnp.ones_like(v)], axis=1)
        r = [_dot(jnp.exp2(s_refs[comp][...] - m_refs[comp][...]).astype(BF16), v1) for comp in range(2)]
        o = r[0][:, :V_DIM] / r[0][:, V_DIM:V_DIM + 1] - lam * (r[1][:, :V_DIM] / r[1][:, V_DIM:V_DIM + 1])
        o_ref[...] = (_rms(o, gsub_ref[...]) * (1.0 - lam_init)).astype(BF16)

    middle = jnp.logical_and(t > 0, t < n_items)

    @pl.when(t == 0)
    def _():
        scores(slots[0])

    for parity in range(2):
        @pl.when(jnp.logical_and(middle, t % 2 == parity))
        def _():
            scores(slots[parity])
            finish(slots[1 - parity])

    @pl.when(t == n_items)
    def _():
        finish(slots[(n_items - 1) % 2])


def _attn(q, kt, v, lamv, gsub, o_prev, *, row0, batch, seq, lam_init):
    t = q.shape[0]
    tq = ATTN_TILE // seq
    b0 = row0 // seq
    q0 = row0 // tq
    nq = seq // tq
    n_items = batch * N_HEADS * nq

    def item(step):
        return step // (N_HEADS * nq), (step // nq) % N_HEADS, step % nq

    def scored(step):
        return item(jnp.minimum(step, n_items - 1))

    def finished(step):
        return item(jnp.maximum(step - 1, 0))

    def q_map(step):
        b, h, i = scored(step)
        return (q0 + b * nq + i, h)

    def kt_map(step):
        b, h, _ = scored(step)
        return (h, b0 + b)

    def v_map(step):
        b, h, _ = finished(step)
        return (b0 + b, h)

    def o_map(step):
        b, h, i = finished(step)
        return (q0 + b * nq + i, h)

    in_specs = [
        _const_spec((4, HEAD_DIM)),
        pl.BlockSpec((tq, LANES), q_map),
        pl.BlockSpec((LANES, seq), kt_map),
        pl.BlockSpec((seq, V_DIM), v_map),
        _const_spec((1, V_DIM)),
    ]
    args = [lamv, q, kt, v, gsub]
    aliases = {}
    if o_prev is not None:
        in_specs.append(pl.BlockSpec(memory_space=pl.ANY))
        args.append(o_prev)
        aliases = {len(args) - 1: 0}
    return pl.pallas_call(
        functools.partial(_attn_kernel, lam_init=lam_init, n_items=n_items),
        grid=(n_items + 1,),
        in_specs=in_specs,
        out_specs=pl.BlockSpec((tq, V_DIM), o_map),
        out_shape=jax.ShapeDtypeStruct((t, ATTN_W), BF16),
        scratch_shapes=[pltpu.VMEM((tq, seq), F32)] * 4 + [pltpu.VMEM((tq, 1), F32)] * 4,
        input_output_aliases=aliases,
        compiler_params=_params(("arbitrary",)),
        name=f"diff_attn_s{seq}",
    )(*args)


def _cmul_const(x, w):
    re, im = x
    wr, wi = round(w.real, 12), round(w.imag, 12)
    if (wr, wi) == (1.0, 0.0):
        return re, im
    if (wr, wi) == (-1.0, 0.0):
        return -re, -im
    if (wr, wi) == (0.0, -1.0):
        return im, -re
    if (wr, wi) == (0.0, 1.0):
        return -im, re
    return re * wr - im * wi, re * wi + im * wr


def _fft(xs):
    n = len(xs)
    if n == 1:
        return xs
    even, odd = _fft(xs[0::2]), _fft(xs[1::2])
    out = [None] * n
    for k in range(n // 2):
        tr, ti = _cmul_const(odd[k], complex(math.cos(2 * math.pi * k / n), -math.sin(2 * math.pi * k / n)))
        er, ei = even[k]
        out[k] = (er + tr, ei + ti)
        out[k + n // 2] = (er - tr, ei - ti)
    return out


def _seqfft_kernel(m1_ref, twc_ref, tws_ref, fab_ref, *rest, scale):
    o_ref, wre, wim = rest[-3:]
    n = m1_ref.shape[1]
    for n2 in range(FFT_RADIX):
        ab = fab_ref[:, n2 * 2 * FOURIER_W:(n2 + 1) * 2 * FOURIER_W]
        p = _dot(m1_ref[...], ab)
        t_re = p[:n, :FOURIER_W] - p[n:, FOURIER_W:]
        t_im = -(p[:n, FOURIER_W:] + p[n:, :FOURIER_W])
        c = jnp.concatenate([twc_ref[n2]] * (FOURIER_W // LANES), axis=1)
        s = jnp.concatenate([tws_ref[n2]] * (FOURIER_W // LANES), axis=1)
        wre[n2] = t_re * c + t_im * s
        wim[n2] = t_im * c - t_re * s
    for g in range(n // F32_ROWS):
        rows = slice(g * F32_ROWS, (g + 1) * F32_ROWS)
        for lb in range(FOURIER_W // LANES):
            cols = slice(lb * LANES, (lb + 1) * LANES)
            ys = _fft([(wre[n2, rows, cols], wim[n2, rows, cols]) for n2 in range(FFT_RADIX)])
            for k2 in range(FFT_RADIX):
                o_ref[k2 * n + g * F32_ROWS:k2 * n + (g + 1) * F32_ROWS, cols] = (ys[k2][0] * scale).astype(BF16)


def _fft_tables(seq):
    n = seq // FFT_RADIX
    cr, sr = _dft_tables(n, n, n)
    twc, tws = _dft_tables(FFT_RADIX, n, seq)
    lanes = lambda a: jnp.broadcast_to(a[:, :, None], (FFT_RADIX, n, LANES))
    return jnp.concatenate([cr, sr], axis=0).astype(BF16), lanes(twc), lanes(tws)


def _seqfft(fab, m1, twc, tws, f_prev, *, row0, batch, seq):
    t = fab.shape[0] * FFT_RADIX
    n = seq // FFT_RADIX
    b0 = row0 // seq
    in_specs = [
        _const_spec((2 * n, n)),
        _const_spec((FFT_RADIX, n, LANES)),
        _const_spec((FFT_RADIX, n, LANES)),
        pl.BlockSpec((n, FFT_RADIX * 2 * FOURIER_W), lambda b: (b0 + b, 0)),
    ]
    args = [m1, twc, tws, fab]
    aliases = {}
    if f_prev is not None:
        in_specs.append(pl.BlockSpec(memory_space=pl.ANY))
        args.append(f_prev)
        aliases = {len(args) - 1: 0}
    return pl.pallas_call(
        functools.partial(_seqfft_kernel, scale=1.0 / math.sqrt(seq * FGROUP_DIM)),
        grid=(batch,),
        in_specs=in_specs,
        out_specs=pl.BlockSpec((seq, FOURIER_W), lambda b: (b0 + b, 0)),
        out_shape=jax.ShapeDtypeStruct((t, FOURIER_W), BF16),
        scratch_shapes=[pltpu.VMEM((FFT_RADIX, n, FOURIER_W), F32)] * 2,
        input_output_aliases=aliases,
        compiler_params=_params(("parallel",)),
        name=f"seq_fft_s{seq}",
    )(*args)


def _post_kernel(x_ref, o_ref, f_ref, gate_ref, wpa_ref, wpf_ref, wo_ref, out_ref):
    br_a = _dot(o_ref[...], wpa_ref[...])
    br_f = _dot(f_ref[...], wpf_ref[...])
    merged = (gate_ref[:, :D_MODEL].astype(F32) * br_a + gate_ref[:, D_MODEL:].astype(F32) * br_f)
    out_ref[...] = x_ref[...] + _dot(merged.astype(BF16), wo_ref[...])


def _post(x, o, f, gate, w_pa, w_pf, w_o, *, layer):
    t = x.shape[0]
    tm = TOKEN_TILE
    return pl.pallas_call(
        _post_kernel,
        grid=(t // tm,),
        in_specs=[
            pl.BlockSpec((tm, D_MODEL), lambda i: (i, 0)),
            pl.BlockSpec((tm, ATTN_W), lambda i: (i, 0)),
            pl.BlockSpec((tm, FOURIER_W), lambda i: (i, 0)),
            pl.BlockSpec((tm, 2 * D_MODEL), lambda i: (i, 0)),
            _layer_spec((ATTN_W, D_MODEL), layer),
            _layer_spec((FOURIER_W, D_MODEL), layer),
            _layer_spec((D_MODEL, D_MODEL), layer),
        ],
        out_specs=pl.BlockSpec((tm, D_MODEL), lambda i: (i, 0)),
        out_shape=jax.ShapeDtypeStruct((t, D_MODEL), F32),
        compiler_params=_params(("parallel",)),
        name="mixer_out",
    )(x, o, f, gate, w_pa, w_pf, w_o)


def _rope_tables(seq):
    inv = 1.0 / (ROPE_THETA ** (jnp.arange(0, HEAD_DIM, 2, dtype=F32) / HEAD_DIM))
    ang = jnp.arange(seq, dtype=F32)[:, None] * inv[None, :]
    ang = jnp.concatenate([ang, ang, ang, ang], axis=-1)
    lane = jnp.arange(LANES)
    sign = jnp.where((lane % HEAD_DIM) < HEAD_DIM // 2, -1.0, 1.0).astype(F32)
    return jnp.cos(ang), jnp.sin(ang) * sign[None, :]


def _dft_tables(rows, cols, n):
    j = jnp.arange(rows, dtype=jnp.int32)
    k = jnp.arange(cols, dtype=jnp.int32)
    ang = ((j[:, None] * k[None, :]) % n).astype(F32) * (2.0 * math.pi / n)
    return jnp.cos(ang), jnp.sin(ang)


def kernel(x_prompt, x_sample, g_ff1, w_ff1_up, w_ff1_down, g_mix, w_in, lam_q1, lam_k1, lam_q2, lam_k2,
           g_sub, w_pa, w_pf, w_gate, w_o, g_ff2, w_ff2_up, w_ff2_down, g_final):
    bp, sp, _ = x_prompt.shape
    bs, ss, _ = x_sample.shape
    n_prompt = bp * sp
    assert n_prompt % ss == 0 and sp <= ss
    n_sample = bs * ss
    n_tok = n_prompt + n_sample

    cos, sin = _rope_tables(ss)
    cg, sg = _dft_tables(FGROUP_DIM, FGROUP_DIM, FGROUP_DIM)
    dft = jnp.concatenate([cg, sg], axis=1).astype(BF16)
    seq_tabs = {s: _fft_tables(s) for s in {sp, ss}}

    row = lambda v: v.reshape(1, -1)
    gf = row(g_final)
    w_ff1_up, w_ff1_down, w_ff2_up, w_ff2_down, w_in, w_gate, w_pa, w_pf, w_o = (
        w.astype(BF16) for w in (w_ff1_up, w_ff1_down, w_ff2_up, w_ff2_down, w_in, w_gate, w_pa, w_pf, w_o))
    x = None
    for l in range(DEPTH):
        lam_init = 0.8 - 0.6 * math.exp(-0.3 * l)
        ff1 = (row(g_ff1[l]), w_ff1_up, w_ff1_down, gf)
        if l == 0:
            x = _ffn(x_prompt.reshape(n_prompt, D_MODEL), *ff1, layer=l, final_norm=False, out_rows=n_tok)
            x = _ffn(x_sample.reshape(n_sample, D_MODEL), *ff1, layer=l, final_norm=False, out_rows=n_tok,
                     out_row0=n_prompt, out_prev=x)
        else:
            x = _ffn(x, *ff1, layer=l, final_norm=False)
        q, kt, v, fab, gate = _pre(x, row(g_mix[l]), w_in, w_gate, cos, sin, dft, layer=l,
                                   n_prompt=n_prompt, s_prompt=sp, s_sample=ss)
        lamv = jnp.stack([lam_q1[l], lam_k1[l], lam_q2[l], lam_k2[l]])
        o = _attn(q, kt, v, lamv, row(g_sub[l]), None, row0=0, batch=bp, seq=sp, lam_init=lam_init)
        o = _attn(q, kt, v, lamv, row(g_sub[l]), o, row0=n_prompt, batch=bs, seq=ss, lam_init=lam_init)
        f = _seqfft(fab, *seq_tabs[sp], None, row0=0, batch=bp, seq=sp)
        f = _seqfft(fab, *seq_tabs[ss], f, row0=n_prompt, batch=bs, seq=ss)
        x = _post(x, o, f, gate, w_pa, w_pf, w_o, layer=l)
        ff2 = (row(g_ff2[l]), w_ff2_up, w_ff2_down, gf)
        if l < DEPTH - 1:
            x = _ffn(x, *ff2, layer=l, final_norm=False)
    last = DEPTH - 1
    y_prompt = _ffn(x, *ff2, layer=last, final_norm=True, rows=n_prompt)
    y_sample = _ffn(x, *ff2, layer=last, final_norm=True, rows=n_sample, in_row0=n_prompt)
    return y_prompt.reshape(bp, sp, D_MODEL), y_sample.reshape(bs, ss, D_MODEL)
```

```python
import functools
import math

import jax
import jax.numpy as jnp
from jax import lax
from jax.experimental import pallas as pl
from jax.experimental.pallas import tpu as pltpu

D_MODEL = 1024
DEPTH = 4
N_HEADS = 4
HEAD_DIM = 64
V_DIM = 2 * HEAD_DIM
QK_W = N_HEADS * 2 * HEAD_DIM
ATTN_W = N_HEADS * V_DIM
N_FGROUPS = 4
FGROUP_DIM = 128
FOURIER_W = N_FGROUPS * FGROUP_DIM
IN_W = 2 * QK_W + ATTN_W + FOURIER_W
D_FF = 2816
ROPE_THETA = 10000.0
EPS = 1e-6

LANES = 128
QKV_W = 2 * QK_W + ATTN_W
Q_SCALE = HEAD_DIM ** -0.5 * math.log2(math.e)
VMEM_LIMIT = 56 * 1024 * 1024
TOKEN_TILE = 512
ATTN_TILE = 256 * 4096
FFT_RADIX = 16
F32_ROWS = 8
PERM_ROWS = 256
ROW_RUN = 256

F32 = jnp.float32
BF16 = jnp.bfloat16


def _rms(x, g):
    return x * lax.rsqrt(jnp.mean(x * x, axis=-1, keepdims=True) + EPS) * g


def _dot(a, b):
    return jnp.dot(a, b, preferred_element_type=F32)


def _params(sem):
    return pltpu.CompilerParams(dimension_semantics=sem, vmem_limit_bytes=VMEM_LIMIT)


def _const_spec(shape):
    return pl.BlockSpec(shape, lambda *_: (0,) * len(shape), pipeline_mode=pl.Buffered(1))


def _layer_spec(shape, layer, col_block=0):
    return pl.BlockSpec((None,) + shape, lambda *_: (layer, 0, col_block), pipeline_mode=pl.Buffered(1))


def _ffn_kernel(x_ref, g_ref, wg_ref, wu_ref, wd_ref, gf_ref, *rest, final_norm):
    o_ref = rest[-1]
    for part in range(x_ref.shape[0] // ROW_RUN):
        rows = slice(part * ROW_RUN, (part + 1) * ROW_RUN)
        x = x_ref[rows, :]
        h = _rms(x, g_ref[...]).astype(BF16)
        gate = _dot(h, wg_ref[...])
        up = _dot(h, wu_ref[...])
        act = (gate / (1.0 + jnp.exp(-gate)) * up).astype(BF16)
        y = x + 0.5 * _dot(act, wd_ref[...])
        if final_norm:
            y = _rms(y, gf_ref[...])
        o_ref[rows, :] = y


def _ffn(x, g, w_up, w_down, g_final, *, layer, final_norm, rows=None, in_row0=0, out_rows=None, out_row0=0,
         out_prev=None):
    tm = TOKEN_TILE
    rows = x.shape[0] if rows is None else rows
    out_rows = rows if out_rows is None else out_rows
    in0, out0 = in_row0 // tm, out_row0 // tm
    in_specs = [
        pl.BlockSpec((tm, D_MODEL), lambda i: (in0 + i, 0)),
        _const_spec((1, D_MODEL)),
        _layer_spec((D_MODEL, D_FF), layer, 0),
        _layer_spec((D_MODEL, D_FF), layer, 1),
        _layer_spec((D_FF, D_MODEL), layer),
        _const_spec((1, D_MODEL)),
    ]
    args = [x, g, w_up, w_up, w_down, g_final]
    aliases = {}
    if out_prev is not None:
        in_specs.append(pl.BlockSpec(memory_space=pl.ANY))
        args.append(out_prev)
        aliases = {len(args) - 1: 0}
    return pl.pallas_call(
        functools.partial(_ffn_kernel, final_norm=final_norm),
        grid=(rows // tm,),
        in_specs=in_specs,
        out_specs=pl.BlockSpec((tm, D_MODEL), lambda i: (out0 + i, 0)),
        out_shape=jax.ShapeDtypeStruct((out_rows, D_MODEL), F32),
        input_output_aliases=aliases,
        compiler_params=_params(("parallel",)),
        name="ffn",
    )(*args)


def _pre_kernel(x_ref, g_ref, win_ref, wgate_ref, cos_ref, sin_ref, dft_ref, perm_ref,
                q_ref, kt_ref, v_ref, fab_ref, gate_ref):
    lane = lax.broadcasted_iota(jnp.int32, (PERM_ROWS, LANES), 1)
    low_half = (lane % HEAD_DIM) < (HEAD_DIM // 2)
    dft = dft_ref[...]
    span = PERM_ROWS // FFT_RADIX
    for part in range(x_ref.shape[0] // PERM_ROWS):
        rows = slice(part * PERM_ROWS, (part + 1) * PERM_ROWS)
        h = _rms(x_ref[rows, :], g_ref[...]).astype(BF16)
        proj = _dot(h, win_ref[...])
        cos = cos_ref[rows, :]
        sin = sin_ref[rows, :]
        for j in range(2 * N_HEADS):
            t = proj[:, j * LANES:(j + 1) * LANES]
            rot = jnp.where(low_half,
                            pltpu.roll(t, LANES - HEAD_DIM // 2, axis=1),
                            pltpu.roll(t, HEAD_DIM // 2, axis=1))
            roped = t * cos + rot * sin
            if j < N_HEADS:
                q_ref[rows, j * LANES:(j + 1) * LANES] = (roped * Q_SCALE).astype(BF16)
            else:
                kt_ref[(j - N_HEADS) * LANES:(j - N_HEADS + 1) * LANES, rows] = roped.T.astype(BF16)
        v_ref[rows, :] = proj[:, 2 * QK_W:QKV_W].astype(BF16)
        up = _dot(perm_ref[...], proj[:, QKV_W:].astype(BF16)).astype(BF16)
        dst = slice(part * span, (part + 1) * span)
        for grp in range(N_FGROUPS):
            ab = _dot(up[:, grp * FGROUP_DIM:(grp + 1) * FGROUP_DIM], dft)
            for n2 in range(FFT_RADIX):
                src = slice(n2 * span, (n2 + 1) * span)
                col = n2 * 2 * FOURIER_W + grp * FGROUP_DIM
                fab_ref[dst, col:col + FGROUP_DIM] = ab[src, :FGROUP_DIM].astype(BF16)
                fab_ref[dst, col + FOURIER_W:col + FOURIER_W + FGROUP_DIM] = ab[src, FGROUP_DIM:].astype(BF16)
        z = _dot(h, wgate_ref[...])
        gate_ref[rows, :] = (1.0 / (1.0 + jnp.exp(-z))).astype(BF16)


def _pre(x, g, w_in, w_gate, cos, sin, dft, *, layer, n_prompt, s_prompt, s_sample):
    t = x.shape[0]
    tm = TOKEN_TILE
    n_prompt_tiles = n_prompt // tm
    idx = jnp.arange(PERM_ROWS)
    src = FFT_RADIX * (idx % (PERM_ROWS // FFT_RADIX)) + idx // (PERM_ROWS // FFT_RADIX)
    perm = (src[:, None] == idx[None, :]).astype(BF16)

    def pos_map(i):
        p = jnp.where(i < n_prompt_tiles, i % (s_prompt // tm), (i - n_prompt_tiles) % (s_sample // tm))
        return (p, 0)

    return pl.pallas_call(
        _pre_kernel,
        grid=(t // tm,),
        in_specs=[
            pl.BlockSpec((tm, D_MODEL), lambda i: (i, 0)),
            _const_spec((1, D_MODEL)),
            _layer_spec((D_MODEL, IN_W), layer),
            _layer_spec((D_MODEL, 2 * D_MODEL), layer),
            pl.BlockSpec((tm, LANES), pos_map),
            pl.BlockSpec((tm, LANES), pos_map),
            _const_spec((FGROUP_DIM, 2 * FGROUP_DIM)),
            _const_spec((PERM_ROWS, PERM_ROWS)),
        ],
        out_specs=[
            pl.BlockSpec((tm, QK_W), lambda i: (i, 0)),
            pl.BlockSpec((QK_W, tm), lambda i: (0, i)),
            pl.BlockSpec((tm, ATTN_W), lambda i: (i, 0)),
            pl.BlockSpec((tm // FFT_RADIX, FFT_RADIX * 2 * FOURIER_W), lambda i: (i, 0)),
            pl.BlockSpec((tm, 2 * D_MODEL), lambda i: (i, 0)),
        ],
        out_shape=[
            jax.ShapeDtypeStruct((t, QK_W), BF16),
            jax.ShapeDtypeStruct((QK_W, t), BF16),
            jax.ShapeDtypeStruct((t, ATTN_W), BF16),
            jax.ShapeDtypeStruct((t // FFT_RADIX, FFT_RADIX * 2 * FOURIER_W), BF16),
            jax.ShapeDtypeStruct((t, 2 * D_MODEL), BF16),
        ],
        compiler_params=_params(("parallel",)),
        name="mixer_in",
    )(x, g, w_in, w_gate, cos, sin, dft, perm)


def _attn_kernel(lamv_ref, q_ref, kt_ref, v_ref, gsub_ref, *rest, lam_init, n_items):
    o_ref = rest[-13]
    scr = rest[-12:]
    slots = tuple((scr[2 * k:2 * k + 2], scr[4 + 2 * k:6 + 2 * k], scr[8 + 2 * k:10 + 2 * k]) for k in range(2))
    t = pl.program_id(0)

    def scores(slot):
        s_refs, m_refs, _ = slots[slot]
        q = q_ref[...]
        kt = kt_ref[...]
        lane = lax.broadcasted_iota(jnp.int32, q.shape, 1)
        zero = jnp.zeros_like(q)
        for comp, qc in enumerate((jnp.where(lane < HEAD_DIM, q, zero), jnp.where(lane >= HEAD_DIM, q, zero))):
            s = _dot(qc, kt)
            s_refs[comp][...] = s
            m_refs[comp][...] = jnp.max(s, axis=-1, keepdims=True)

    def values(slot):
        s_refs, m_refs, r_refs = slots[slot]
        v = v_ref[...]
        v1 = jnp.concatenate([v, jnp.ones_like(v)], axis=1)
        for comp in range(2):
            r_refs[comp][...] = _dot(jnp.exp2(s_refs[comp][...] - m_refs[comp][...]).astype(BF16), v1)

    def output(slot):
        r0, r1 = (r[...] for r in slots[slot][2])
        lamv = lamv_ref[...]
        lam = (jnp.exp(jnp.sum(lamv[0:1] * lamv[1:2], axis=-1, keepdims=True))
               - jnp.exp(jnp.sum(lamv[2:3] * lamv[3:4], axis=-1, keepdims=True)) + lam_init)
        o = r0[:, :V_DIM] / r0[:, V_DIM:V_DIM + 1] - lam * (r1[:, :V_DIM] / r1[:, V_DIM:V_DIM + 1])
        o_ref[...] = (_rms(o, gsub_ref[...]) * (1.0 - lam_init)).astype(BF16)

    def stages(parity, do_scores, do_values, do_output):
        if do_output:
            output(parity)
        if do_scores:
            scores(parity)
        if do_values:
            values(1 - parity)

    @pl.when(t == 0)
    def _():
        stages(0, True, False, False)

    @pl.when(t == 1)
    def _():
        stages(1, True, True, False)

    for parity in range(2):
        @pl.when(jnp.logical_and(jnp.logical_and(t >= 2, t < n_items), t % 2 == parity))
        def _():
            stages(parity, True, True, True)

    @pl.when(t == n_items)
    def _():
        stages(n_items % 2, False, True, True)

    @pl.when(t == n_items + 1)
    def _():
        stages((n_items + 1) % 2, False, False, True)


def _attn(q, kt, v, lamv, gsub, o_prev, *, row0, batch, seq, lam_init):
    t = q.shape[0]
    tq = ATTN_TILE // seq
    b0 = row0 // seq
    q0 = row0 // tq
    nq = seq // tq
    n_items = batch * N_HEADS * nq

    def item(step):
        step = jnp.clip(step, 0, n_items - 1)
        return step // (N_HEADS * nq), (step // nq) % N_HEADS, step % nq

    def q_map(step):
        b, h, i = item(step)
        return (q0 + b * nq + i, h)

    def kt_map(step):
        b, h, _ = item(step)
        return (h, b0 + b)

    def v_map(step):
        b, h, _ = item(step - 1)
        return (b0 + b, h)

    def o_map(step):
        b, h, i = item(step - 2)
        return (q0 + b * nq + i, h)

    in_specs = [
        _const_spec((4, HEAD_DIM)),
        pl.BlockSpec((tq, LANES), q_map),
        pl.BlockSpec((LANES, seq), kt_map),
        pl.BlockSpec((seq, V_DIM), v_map),
        _const_spec((1, V_DIM)),
    ]
    args = [lamv, q, kt, v, gsub]
    aliases = {}
    if o_prev is not None:
        in_specs.append(pl.BlockSpec(memory_space=pl.ANY))
        args.append(o_prev)
        aliases = {len(args) - 1: 0}
    return pl.pallas_call(
        functools.partial(_attn_kernel, lam_init=lam_init, n_items=n_items),
        grid=(n_items + 2,),
        in_specs=in_specs,
        out_specs=pl.BlockSpec((tq, V_DIM), o_map),
        out_shape=jax.ShapeDtypeStruct((t, ATTN_W), BF16),
        scratch_shapes=([pltpu.VMEM((tq, seq), F32)] * 4 + [pltpu.VMEM((tq, 1), F32)] * 4
                        + [pltpu.VMEM((tq, 2 * V_DIM), F32)] * 4),
        input_output_aliases=aliases,
        compiler_params=_params(("arbitrary",)),
        name=f"diff_attn_s{seq}",
    )(*args)


def _cmul_const(x, w):
    re, im = x
    wr, wi = round(w.real, 12), round(w.imag, 12)
    if (wr, wi) == (1.0, 0.0):
        return re, im
    if (wr, wi) == (-1.0, 0.0):
        return -re, -im
    if (wr, wi) == (0.0, -1.0):
        return im, -re
    if (wr, wi) == (0.0, 1.0):
        return -im, re
    return re * wr - im * wi, re * wi + im * wr


def _fft(xs):
    n = len(xs)
    if n == 1:
        return xs
    even, odd = _fft(xs[0::2]), _fft(xs[1::2])
    out = [None] * n
    for k in range(n // 2):
        tr, ti = _cmul_const(odd[k], complex(math.cos(2 * math.pi * k / n), -math.sin(2 * math.pi * k / n)))
        er, ei = even[k]
        out[k] = (er + tr, ei + ti)
        out[k + n // 2] = (er - tr, ei - ti)
    return out


def _seqfft_kernel(m1_ref, twc_ref, tws_ref, fab_ref, *rest, scale):
    o_ref, wre, wim = rest[-3:]
    n = m1_ref.shape[1]
    for n2 in range(FFT_RADIX):
        ab = fab_ref[:, n2 * 2 * FOURIER_W:(n2 + 1) * 2 * FOURIER_W]
        p = _dot(m1_ref[...], ab)
        t_re = p[:n, :FOURIER_W] - p[n:, FOURIER_W:]
        t_im = -(p[:n, FOURIER_W:] + p[n:, :FOURIER_W])
        c = jnp.concatenate([twc_ref[n2]] * (FOURIER_W // LANES), axis=1)
        s = jnp.concatenate([tws_ref[n2]] * (FOURIER_W // LANES), axis=1)
        wre[n2] = t_re * c + t_im * s
        wim[n2] = t_im * c - t_re * s
    for g in range(n // F32_ROWS):
        rows = slice(g * F32_ROWS, (g + 1) * F32_ROWS)
        for lb in range(FOURIER_W // LANES):
            cols = slice(lb * LANES, (lb + 1) * LANES)
            ys = _fft([(wre[n2, rows, cols], wim[n2, rows, cols]) for n2 in range(FFT_RADIX)])
            for k2 in range(FFT_RADIX):
                o_ref[k2 * n + g * F32_ROWS:k2 * n + (g + 1) * F32_ROWS, cols] = (ys[k2][0] * scale).astype(BF16)


def _fft_tables(seq):
    n = seq // FFT_RADIX
    cr, sr = _dft_tables(n, n, n)
    twc, tws = _dft_tables(FFT_RADIX, n, seq)
    lanes = lambda a: jnp.broadcast_to(a[:, :, None], (FFT_RADIX, n, LANES))
    return jnp.concatenate([cr, sr], axis=0).astype(BF16), lanes(twc), lanes(tws)


def _seqfft(fab, m1, twc, tws, f_prev, *, row0, batch, seq):
    t = fab.shape[0] * FFT_RADIX
    n = seq // FFT_RADIX
    b0 = row0 // seq
    in_specs = [
        _const_spec((2 * n, n)),
        _const_spec((FFT_RADIX, n, LANES)),
        _const_spec((FFT_RADIX, n, LANES)),
        pl.BlockSpec((n, FFT_RADIX * 2 * FOURIER_W), lambda b: (b0 + b, 0)),
    ]
    args = [m1, twc, tws, fab]
    aliases = {}
    if f_prev is not None:
        in_specs.append(pl.BlockSpec(memory_space=pl.ANY))
        args.append(f_prev)
        aliases = {len(args) - 1: 0}
    return pl.pallas_call(
        functools.partial(_seqfft_kernel, scale=1.0 / math.sqrt(seq * FGROUP_DIM)),
        grid=(batch,),
        in_specs=in_specs,
        out_specs=pl.BlockSpec((seq, FOURIER_W), lambda b: (b0 + b, 0)),
        out_shape=jax.ShapeDtypeStruct((t, FOURIER_W), BF16),
        scratch_shapes=[pltpu.VMEM((FFT_RADIX, n, FOURIER_W), F32)] * 2,
        input_output_aliases=aliases,
        compiler_params=_params(("parallel",)),
        name=f"seq_fft_s{seq}",
    )(*args)


def _post_kernel(x_ref, o_ref, f_ref, gate_ref, wpa_ref, wpf_ref, wo_ref, out_ref):
    br_a = _dot(o_ref[...], wpa_ref[...])
    br_f = _dot(f_ref[...], wpf_ref[...])
    merged = (gate_ref[:, :D_MODEL].astype(F32) * br_a + gate_ref[:, D_MODEL:].astype(F32) * br_f)
    out_ref[...] = x_ref[...] + _dot(merged.astype(BF16), wo_ref[...])


def _post(x, o, f, gate, w_pa, w_pf, w_o, *, layer):
    t = x.shape[0]
    tm = TOKEN_TILE
    return pl.pallas_call(
        _post_kernel,
        grid=(t // tm,),
        in_specs=[
            pl.BlockSpec((tm, D_MODEL), lambda i: (i, 0)),
            pl.BlockSpec((tm, ATTN_W), lambda i: (i, 0)),
            pl.BlockSpec((tm, FOURIER_W), lambda i: (i, 0)),
            pl.BlockSpec((tm, 2 * D_MODEL), lambda i: (i, 0)),
            _layer_spec((ATTN_W, D_MODEL), layer),
            _layer_spec((FOURIER_W, D_MODEL), layer),
            _layer_spec((D_MODEL, D_MODEL), layer),
        ],
        out_specs=pl.BlockSpec((tm, D_MODEL), lambda i: (i, 0)),
        out_shape=jax.ShapeDtypeStruct((t, D_MODEL), F32),
        compiler_params=_params(("parallel",)),
        name="mixer_out",
    )(x, o, f, gate, w_pa, w_pf, w_o)


def _rope_tables(seq):
    inv = 1.0 / (ROPE_THETA ** (jnp.arange(0, HEAD_DIM, 2, dtype=F32) / HEAD_DIM))
    ang = jnp.arange(seq, dtype=F32)[:, None] * inv[None, :]
    ang = jnp.concatenate([ang, ang, ang, ang], axis=-1)
    lane = jnp.arange(LANES)
    sign = jnp.where((lane % HEAD_DIM) < HEAD_DIM // 2, -1.0, 1.0).astype(F32)
    return jnp.cos(ang), jnp.sin(ang) * sign[None, :]


def _dft_tables(rows, cols, n):
    j = jnp.arange(rows, dtype=jnp.int32)
    k = jnp.arange(cols, dtype=jnp.int32)
    ang = ((j[:, None] * k[None, :]) % n).astype(F32) * (2.0 * math.pi / n)
    return jnp.cos(ang), jnp.sin(ang)


def kernel(x_prompt, x_sample, g_ff1, w_ff1_up, w_ff1_down, g_mix, w_in, lam_q1, lam_k1, lam_q2, lam_k2,
           g_sub, w_pa, w_pf, w_gate, w_o, g_ff2, w_ff2_up, w_ff2_down, g_final):
    bp, sp, _ = x_prompt.shape
    bs, ss, _ = x_sample.shape
    n_prompt = bp * sp
    assert n_prompt % ss == 0 and sp <= ss
    n_sample = bs * ss
    n_tok = n_prompt + n_sample

    cos, sin = _rope_tables(ss)
    cg, sg = _dft_tables(FGROUP_DIM, FGROUP_DIM, FGROUP_DIM)
    dft = jnp.concatenate([cg, sg], axis=1).astype(BF16)
    seq_tabs = {s: _fft_tables(s) for s in {sp, ss}}

    row = lambda v: v.reshape(1, -1)
    gf = row(g_final)
    w_ff1_up, w_ff1_down, w_ff2_up, w_ff2_down, w_in, w_gate, w_pa, w_pf, w_o = (
        w.astype(BF16) for w in (w_ff1_up, w_ff1_down, w_ff2_up, w_ff2_down, w_in, w_gate, w_pa, w_pf, w_o))
    x = None
    for l in range(DEPTH):
        lam_init = 0.8 - 0.6 * math.exp(-0.3 * l)
        ff1 = (row(g_ff1[l]), w_ff1_up, w_ff1_down, gf)
        if l == 0:
            x = _ffn(x_prompt.reshape(n_prompt, D_MODEL), *ff1, layer=l, final_norm=False, out_rows=n_tok)
            x = _ffn(x_sample.reshape(n_sample, D_MODEL), *ff1, layer=l, final_norm=False, out_rows=n_tok,
                     out_row0=n_prompt, out_prev=x)
        else:
            x = _ffn(x, *ff1, layer=l, final_norm=False)
        q, kt, v, fab, gate = _pre(x, row(g_mix[l]), w_in, w_gate, cos, sin, dft, layer=l,
                                   n_prompt=n_prompt, s_prompt=sp, s_sample=ss)
        lamv = jnp.stack([lam_q1[l], lam_k1[l], lam_q2[l], lam_k2[l]])
        o = _attn(q, kt, v, lamv, row(g_sub[l]), None, row0=0, batch=bp, seq=sp, lam_init=lam_init)
        o = _attn(q, kt, v, lamv, row(g_sub[l]), o, row0=n_prompt, batch=bs, seq=ss, lam_init=lam_init)
        f = _seqfft(fab, *seq_tabs[sp], None, row0=0, batch=bp, seq=sp)
        f = _seqfft(fab, *seq_tabs[ss], f, row0=n_prompt, batch=bs, seq=ss)
        x = _post(x, o, f, gate, w_pa, w_pf, w_o, layer=l)
        ff2 = (row(g_ff2[l]), w_ff2_up, w_ff2_down, gf)
        if l < DEPTH - 1:
            x = _ffn(x, *ff2, layer=l, final_norm=False)
    last = DEPTH - 1
    y_prompt = _ffn(x, *ff2, layer=last, final_norm=True, rows=n_prompt)
    y_sample = _ffn(x, *ff2, layer=last, final_norm=True, rows=n_sample, in_row0=n_prompt)
    return y_prompt.reshape(bp, sp, D_MODEL), y_sample.reshape(bs, ss, D_MODEL)
```

```python
import functools
import math

import jax
import jax.numpy as jnp
from jax import lax
from jax.experimental import pallas as pl
from jax.experimental.pallas import tpu as pltpu

D_MODEL = 1024
DEPTH = 4
N_HEADS = 4
HEAD_DIM = 64
V_DIM = 2 * HEAD_DIM
QK_W = N_HEADS * 2 * HEAD_DIM
ATTN_W = N_HEADS * V_DIM
N_FGROUPS = 4
FGROUP_DIM = 128
FOURIER_W = N_FGROUPS * FGROUP_DIM
IN_W = 2 * QK_W + ATTN_W + FOURIER_W
D_FF = 2816
ROPE_THETA = 10000.0
EPS = 1e-6

LANES = 128
QKV_W = 2 * QK_W + ATTN_W
Q_SCALE = HEAD_DIM ** -0.5 * math.log2(math.e)
VMEM_LIMIT = 56 * 1024 * 1024
TOKEN_TILE = 512
FFN_TILE = 1024
ATTN_TILE = 512 * 4096
FFT_RADIX = 16
F32_ROWS = 8
PERM_ROWS = 256
ROW_RUN = 256

F32 = jnp.float32
BF16 = jnp.bfloat16


def _rms(x, g):
    return x * lax.rsqrt(jnp.mean(x * x, axis=-1, keepdims=True) + EPS) * g


def _dot(a, b):
    return jnp.dot(a, b, preferred_element_type=F32)


def _params(sem):
    return pltpu.CompilerParams(dimension_semantics=sem, vmem_limit_bytes=VMEM_LIMIT)


def _const_spec(shape):
    return pl.BlockSpec(shape, lambda *_: (0,) * len(shape), pipeline_mode=pl.Buffered(1))


def _layer_spec(shape, layer, col_block=0):
    return pl.BlockSpec((None,) + shape, lambda *_: (layer, 0, col_block), pipeline_mode=pl.Buffered(1))


def _ffn_kernel(x_ref, g_ref, wg_ref, wu_ref, wd_ref, gf_ref, *rest, final_norm):
    o_ref = rest[-1]
    for part in range(x_ref.shape[0] // ROW_RUN):
        rows = slice(part * ROW_RUN, (part + 1) * ROW_RUN)
        x = x_ref[rows, :]
        h = _rms(x, g_ref[...]).astype(BF16)
        gate = _dot(h, wg_ref[...])
        up = _dot(h, wu_ref[...])
        act = (gate / (1.0 + jnp.exp(-gate)) * up).astype(BF16)
        y = x + 0.5 * _dot(act, wd_ref[...])
        if final_norm:
            y = _rms(y, gf_ref[...])
        o_ref[rows, :] = y


def _ffn(x, g, w_up, w_down, g_final, *, layer, final_norm, rows=None, in_row0=0, out_rows=None, out_row0=0,
         out_prev=None):
    tm = FFN_TILE
    rows = x.shape[0] if rows is None else rows
    out_rows = rows if out_rows is None else out_rows
    in0, out0 = in_row0 // tm, out_row0 // tm
    in_specs = [
        pl.BlockSpec((tm, D_MODEL), lambda i: (in0 + i, 0)),
        _const_spec((1, D_MODEL)),
        _layer_spec((D_MODEL, D_FF), layer, 0),
        _layer_spec((D_MODEL, D_FF), layer, 1),
        _layer_spec((D_FF, D_MODEL), layer),
        _const_spec((1, D_MODEL)),
    ]
    args = [x, g, w_up, w_up, w_down, g_final]
    aliases = {}
    if out_prev is not None:
        in_specs.append(pl.BlockSpec(memory_space=pl.ANY))
        args.append(out_prev)
        aliases = {len(args) - 1: 0}
    return pl.pallas_call(
        functools.partial(_ffn_kernel, final_norm=final_norm),
        grid=(rows // tm,),
        in_specs=in_specs,
        out_specs=pl.BlockSpec((tm, D_MODEL), lambda i: (out0 + i, 0)),
        out_shape=jax.ShapeDtypeStruct((out_rows, D_MODEL), F32),
        input_output_aliases=aliases,
        compiler_params=_params(("parallel",)),
        name="ffn",
    )(*args)


def _pre_kernel(x_ref, g_ref, win_ref, wgate_ref, cos_ref, sin_ref, dft_ref, perm_ref,
                q_ref, kt_ref, v_ref, fab_ref, gate_ref):
    lane = lax.broadcasted_iota(jnp.int32, (PERM_ROWS, LANES), 1)
    low_half = (lane % HEAD_DIM) < (HEAD_DIM // 2)
    dft = dft_ref[...]
    span = PERM_ROWS // FFT_RADIX
    for part in range(x_ref.shape[0] // PERM_ROWS):
        rows = slice(part * PERM_ROWS, (part + 1) * PERM_ROWS)
        h = _rms(x_ref[rows, :], g_ref[...]).astype(BF16)
        proj = _dot(h, win_ref[...])
        cos = cos_ref[rows, :]
        sin = sin_ref[rows, :]
        for j in range(2 * N_HEADS):
            t = proj[:, j * LANES:(j + 1) * LANES]
            rot = jnp.where(low_half,
                            pltpu.roll(t, LANES - HEAD_DIM // 2, axis=1),
                            pltpu.roll(t, HEAD_DIM // 2, axis=1))
            roped = t * cos + rot * sin
            if j < N_HEADS:
                q_ref[rows, j * LANES:(j + 1) * LANES] = (roped * Q_SCALE).astype(BF16)
            else:
                kt_ref[(j - N_HEADS) * LANES:(j - N_HEADS + 1) * LANES, rows] = roped.T.astype(BF16)
        v_ref[rows, :] = proj[:, 2 * QK_W:QKV_W].astype(BF16)
        up = _dot(perm_ref[...], proj[:, QKV_W:].astype(BF16)).astype(BF16)
        dst = slice(part * span, (part + 1) * span)
        for grp in range(N_FGROUPS):
            ab = _dot(up[:, grp * FGROUP_DIM:(grp + 1) * FGROUP_DIM], dft)
            for n2 in range(FFT_RADIX):
                src = slice(n2 * span, (n2 + 1) * span)
                col = n2 * 2 * FOURIER_W + grp * FGROUP_DIM
                fab_ref[dst, col:col + FGROUP_DIM] = ab[src, :FGROUP_DIM].astype(BF16)
                fab_ref[dst, col + FOURIER_W:col + FOURIER_W + FGROUP_DIM] = ab[src, FGROUP_DIM:].astype(BF16)
        z = _dot(h, wgate_ref[...])
        gate_ref[rows, :] = (1.0 / (1.0 + jnp.exp(-z))).astype(BF16)


def _pre(x, g, w_in, w_gate, cos, sin, dft, *, layer, n_prompt, s_prompt, s_sample):
    t = x.shape[0]
    tm = TOKEN_TILE
    n_prompt_tiles = n_prompt // tm
    idx = jnp.arange(PERM_ROWS)
    src = FFT_RADIX * (idx % (PERM_ROWS // FFT_RADIX)) + idx // (PERM_ROWS // FFT_RADIX)
    perm = (src[:, None] == idx[None, :]).astype(BF16)

    def pos_map(i):
        p = jnp.where(i < n_prompt_tiles, i % (s_prompt // tm), (i - n_prompt_tiles) % (s_sample // tm))
        return (p, 0)

    return pl.pallas_call(
        _pre_kernel,
        grid=(t // tm,),
        in_specs=[
            pl.BlockSpec((tm, D_MODEL), lambda i: (i, 0)),
            _const_spec((1, D_MODEL)),
            _layer_spec((D_MODEL, IN_W), layer),
            _layer_spec((D_MODEL, 2 * D_MODEL), layer),
            pl.BlockSpec((tm, LANES), pos_map),
            pl.BlockSpec((tm, LANES), pos_map),
            _const_spec((FGROUP_DIM, 2 * FGROUP_DIM)),
            _const_spec((PERM_ROWS, PERM_ROWS)),
        ],
        out_specs=[
            pl.BlockSpec((tm, QK_W), lambda i: (i, 0)),
            pl.BlockSpec((QK_W, tm), lambda i: (0, i)),
            pl.BlockSpec((tm, ATTN_W), lambda i: (i, 0)),
            pl.BlockSpec((tm // FFT_RADIX, FFT_RADIX * 2 * FOURIER_W), lambda i: (i, 0)),
            pl.BlockSpec((tm, 2 * D_MODEL), lambda i: (i, 0)),
        ],
        out_shape=[
            jax.ShapeDtypeStruct((t, QK_W), BF16),
            jax.ShapeDtypeStruct((QK_W, t), BF16),
            jax.ShapeDtypeStruct((t, ATTN_W), BF16),
            jax.ShapeDtypeStruct((t // FFT_RADIX, FFT_RADIX * 2 * FOURIER_W), BF16),
            jax.ShapeDtypeStruct((t, 2 * D_MODEL), BF16),
        ],
        compiler_params=_params(("parallel",)),
        name="mixer_in",
    )(x, g, w_in, w_gate, cos, sin, dft, perm)


def _attn_kernel(lamv_ref, q_ref, kt_ref, v_ref, gsub_ref, *rest, lam_init, n_items):
    o_ref = rest[-13]
    scr = rest[-12:]
    slots = tuple((scr[2 * k:2 * k + 2], scr[4 + 2 * k:6 + 2 * k], scr[8 + 2 * k:10 + 2 * k]) for k in range(2))
    t = pl.program_id(0)

    def scores(slot):
        s_refs, m_refs, _ = slots[slot]
        q = q_ref[...]
        kt = kt_ref[...]
        lane = lax.broadcasted_iota(jnp.int32, q.shape, 1)
        zero = jnp.zeros_like(q)
        for comp, qc in enumerate((jnp.where(lane < HEAD_DIM, q, zero), jnp.where(lane >= HEAD_DIM, q, zero))):
            s = _dot(qc, kt)
            s_refs[comp][...] = s
            m_refs[comp][...] = jnp.max(s, axis=-1, keepdims=True)

    def values(slot):
        s_refs, m_refs, r_refs = slots[slot]
        v = v_ref[...]
        v1 = jnp.concatenate([v, jnp.ones_like(v)], axis=1)
        for comp in range(2):
            r_refs[comp][...] = _dot(jnp.exp2(s_refs[comp][...] - m_refs[comp][...]).astype(BF16), v1)

    def output(slot):
        r0, r1 = (r[...] for r in slots[slot][2])
        lamv = lamv_ref[...]
        lam = (jnp.exp(jnp.sum(lamv[0:1] * lamv[1:2], axis=-1, keepdims=True))
               - jnp.exp(jnp.sum(lamv[2:3] * lamv[3:4], axis=-1, keepdims=True)) + lam_init)
        o = r0[:, :V_DIM] / r0[:, V_DIM:V_DIM + 1] - lam * (r1[:, :V_DIM] / r1[:, V_DIM:V_DIM + 1])
        o_ref[...] = (_rms(o, gsub_ref[...]) * (1.0 - lam_init)).astype(BF16)

    def stages(parity, do_scores, do_values, do_output):
        if do_output:
            output(parity)
        if do_scores:
            scores(parity)
        if do_values:
            values(1 - parity)

    @pl.when(t == 0)
    def _():
        stages(0, True, False, False)

    @pl.when(t == 1)
    def _():
        stages(1, True, True, False)

    for parity in range(2):
        @pl.when(jnp.logical_and(jnp.logical_and(t >= 2, t < n_items), t % 2 == parity))
        def _():
            stages(parity, True, True, True)

    @pl.when(t == n_items)
    def _():
        stages(n_items % 2, False, True, True)

    @pl.when(t == n_items + 1)
    def _():
        stages((n_items + 1) % 2, False, False, True)


def _attn(q, kt, v, lamv, gsub, o_prev, *, row0, batch, seq, lam_init):
    t = q.shape[0]
    tq = ATTN_TILE // seq
    b0 = row0 // seq
    q0 = row0 // tq
    nq = seq // tq
    n_items = batch * N_HEADS * nq

    def item(step):
        step = jnp.clip(step, 0, n_items - 1)
        return step // (N_HEADS * nq), (step // nq) % N_HEADS, step % nq

    def q_map(step):
        b, h, i = item(step)
        return (q0 + b * nq + i, h)

    def kt_map(step):
        b, h, _ = item(step)
        return (h, b0 + b)

    def v_map(step):
        b, h, _ = item(step - 1)
        return (b0 + b, h)

    def o_map(step):
        b, h, i = item(step - 2)
        return (q0 + b * nq + i, h)

    in_specs = [
        _const_spec((4, HEAD_DIM)),
        pl.BlockSpec((tq, LANES), q_map),
        pl.BlockSpec((LANES, seq), kt_map),
        pl.BlockSpec((seq, V_DIM), v_map),
        _const_spec((1, V_DIM)),
    ]
    args = [lamv, q, kt, v, gsub]
    aliases = {}
    if o_prev is not None:
        in_specs.append(pl.BlockSpec(memory_space=pl.ANY))
        args.append(o_prev)
        aliases = {len(args) - 1: 0}
    return pl.pallas_call(
        functools.partial(_attn_kernel, lam_init=lam_init, n_items=n_items),
        grid=(n_items + 2,),
        in_specs=in_specs,
        out_specs=pl.BlockSpec((tq, V_DIM), o_map),
        out_shape=jax.ShapeDtypeStruct((t, ATTN_W), BF16),
        scratch_shapes=([pltpu.VMEM((tq, seq), F32)] * 4 + [pltpu.VMEM((tq, 1), F32)] * 4
                        + [pltpu.VMEM((tq, 2 * V_DIM), F32)] * 4),
        input_output_aliases=aliases,
        compiler_params=_params(("arbitrary",)),
        name=f"diff_attn_s{seq}",
    )(*args)


def _cmul_const(x, w):
    re, im = x
    wr, wi = round(w.real, 12), round(w.imag, 12)
    if (wr, wi) == (1.0, 0.0):
        return re, im
    if (wr, wi) == (-1.0, 0.0):
        return -re, -im
    if (wr, wi) == (0.0, -1.0):
        return im, -re
    if (wr, wi) == (0.0, 1.0):
        return -im, re
    return re * wr - im * wi, re * wi + im * wr


def _fft(xs):
    n = len(xs)
    if n == 1:
        return xs
    even, odd = _fft(xs[0::2]), _fft(xs[1::2])
    out = [None] * n
    for k in range(n // 2):
        tr, ti = _cmul_const(odd[k], complex(math.cos(2 * math.pi * k / n), -math.sin(2 * math.pi * k / n)))
        er, ei = even[k]
        out[k] = (er + tr, ei + ti)
        out[k + n // 2] = (er - tr, ei - ti)
    return out


def _seqfft_kernel(m1_ref, twc_ref, tws_ref, fab_ref, *rest, scale):
    o_ref, wre, wim = rest[-3:]
    n = m1_ref.shape[1]
    for n2 in range(FFT_RADIX):
        ab = fab_ref[:, n2 * 2 * FOURIER_W:(n2 + 1) * 2 * FOURIER_W]
        p = _dot(m1_ref[...], ab)
        t_re = p[:n, :FOURIER_W] - p[n:, FOURIER_W:]
        t_im = -(p[:n, FOURIER_W:] + p[n:, :FOURIER_W])
        c = jnp.concatenate([twc_ref[n2]] * (FOURIER_W // LANES), axis=1)
        s = jnp.concatenate([tws_ref[n2]] * (FOURIER_W // LANES), axis=1)
        wre[n2] = t_re * c + t_im * s
        wim[n2] = t_im * c - t_re * s
    for g in range(n // F32_ROWS):
        rows = slice(g * F32_ROWS, (g + 1) * F32_ROWS)
        for lb in range(FOURIER_W // LANES):
            cols = slice(lb * LANES, (lb + 1) * LANES)
            ys = _fft([(wre[n2, rows, cols], wim[n2, rows, cols]) for n2 in range(FFT_RADIX)])
            for k2 in range(FFT_RADIX):
                o_ref[k2 * n + g * F32_ROWS:k2 * n + (g + 1) * F32_ROWS, cols] = (ys[k2][0] * scale).astype(BF16)


def _fft_tables(seq):
    n = seq // FFT_RADIX
    cr, sr = _dft_tables(n, n, n)
    twc, tws = _dft_tables(FFT_RADIX, n, seq)
    lanes = lambda a: jnp.broadcast_to(a[:, :, None], (FFT_RADIX, n, LANES))
    return jnp.concatenate([cr, sr], axis=0).astype(BF16), lanes(twc), lanes(tws)


def _seqfft(fab, m1, twc, tws, f_prev, *, row0, batch, seq):
    t = fab.shape[0] * FFT_RADIX
    n = seq // FFT_RADIX
    b0 = row0 // seq
    in_specs = [
        _const_spec((2 * n, n)),
        _const_spec((FFT_RADIX, n, LANES)),
        _const_spec((FFT_RADIX, n, LANES)),
        pl.BlockSpec((n, FFT_RADIX * 2 * FOURIER_W), lambda b: (b0 + b, 0)),
    ]
    args = [m1, twc, tws, fab]
    aliases = {}
    if f_prev is not None:
        in_specs.append(pl.BlockSpec(memory_space=pl.ANY))
        args.append(f_prev)
        aliases = {len(args) - 1: 0}
    return pl.pallas_call(
        functools.partial(_seqfft_kernel, scale=1.0 / math.sqrt(seq * FGROUP_DIM)),
        grid=(batch,),
        in_specs=in_specs,
        out_specs=pl.BlockSpec((seq, FOURIER_W), lambda b: (b0 + b, 0)),
        out_shape=jax.ShapeDtypeStruct((t, FOURIER_W), BF16),
        scratch_shapes=[pltpu.VMEM((FFT_RADIX, n, FOURIER_W), F32)] * 2,
        input_output_aliases=aliases,
        compiler_params=_params(("parallel",)),
        name=f"seq_fft_s{seq}",
    )(*args)


def _post_kernel(x_ref, o_ref, f_ref, gate_ref, wpa_ref, wpf_ref, wo_ref, out_ref):
    br_a = _dot(o_ref[...], wpa_ref[...])
    br_f = _dot(f_ref[...], wpf_ref[...])
    merged = (gate_ref[:, :D_MODEL].astype(F32) * br_a + gate_ref[:, D_MODEL:].astype(F32) * br_f)
    out_ref[...] = x_ref[...] + _dot(merged.astype(BF16), wo_ref[...])


def _post(x, o, f, gate, w_pa, w_pf, w_o, *, layer):
    t = x.shape[0]
    tm = TOKEN_TILE
    return pl.pallas_call(
        _post_kernel,
        grid=(t // tm,),
        in_specs=[
            pl.BlockSpec((tm, D_MODEL), lambda i: (i, 0)),
            pl.BlockSpec((tm, ATTN_W), lambda i: (i, 0)),
            pl.BlockSpec((tm, FOURIER_W), lambda i: (i, 0)),
            pl.BlockSpec((tm, 2 * D_MODEL), lambda i: (i, 0)),
            _layer_spec((ATTN_W, D_MODEL), layer),
            _layer_spec((FOURIER_W, D_MODEL), layer),
            _layer_spec((D_MODEL, D_MODEL), layer),
        ],
        out_specs=pl.BlockSpec((tm, D_MODEL), lambda i: (i, 0)),
        out_shape=jax.ShapeDtypeStruct((t, D_MODEL), F32),
        compiler_params=_params(("parallel",)),
        name="mixer_out",
    )(x, o, f, gate, w_pa, w_pf, w_o)


def _rope_tables(seq):
    inv = 1.0 / (ROPE_THETA ** (jnp.arange(0, HEAD_DIM, 2, dtype=F32) / HEAD_DIM))
    ang = jnp.arange(seq, dtype=F32)[:, None] * inv[None, :]
    ang = jnp.concatenate([ang, ang, ang, ang], axis=-1)
    lane = jnp.arange(LANES)
    sign = jnp.where((lane % HEAD_DIM) < HEAD_DIM // 2, -1.0, 1.0).astype(F32)
    return jnp.cos(ang), jnp.sin(ang) * sign[None, :]


def _dft_tables(rows, cols, n):
    j = jnp.arange(rows, dtype=jnp.int32)
    k = jnp.arange(cols, dtype=jnp.int32)
    ang = ((j[:, None] * k[None, :]) % n).astype(F32) * (2.0 * math.pi / n)
    return jnp.cos(ang), jnp.sin(ang)


def kernel(x_prompt, x_sample, g_ff1, w_ff1_up, w_ff1_down, g_mix, w_in, lam_q1, lam_k1, lam_q2, lam_k2,
           g_sub, w_pa, w_pf, w_gate, w_o, g_ff2, w_ff2_up, w_ff2_down, g_final):
    bp, sp, _ = x_prompt.shape
    bs, ss, _ = x_sample.shape
    n_prompt = bp * sp
    assert n_prompt % ss == 0 and sp <= ss
    n_sample = bs * ss
    n_tok = n_prompt + n_sample

    cos, sin = _rope_tables(ss)
    cg, sg = _dft_tables(FGROUP_DIM, FGROUP_DIM, FGROUP_DIM)
    dft = jnp.concatenate([cg, sg], axis=1).astype(BF16)
    seq_tabs = {s: _fft_tables(s) for s in {sp, ss}}

    row = lambda v: v.reshape(1, -1)
    gf = row(g_final)
    w_ff1_up, w_ff1_down, w_ff2_up, w_ff2_down, w_in, w_gate, w_pa, w_pf, w_o = (
        w.astype(BF16) for w in (w_ff1_up, w_ff1_down, w_ff2_up, w_ff2_down, w_in, w_gate, w_pa, w_pf, w_o))
    x = None
    for l in range(DEPTH):
        lam_init = 0.8 - 0.6 * math.exp(-0.3 * l)
        ff1 = (row(g_ff1[l]), w_ff1_up, w_ff1_down, gf)
        if l == 0:
            x = _ffn(x_prompt.reshape(n_prompt, D_MODEL), *ff1, layer=l, final_norm=False, out_rows=n_tok)
            x = _ffn(x_sample.reshape(n_sample, D_MODEL), *ff1, layer=l, final_norm=False, out_rows=n_tok,
                     out_row0=n_prompt, out_prev=x)
        else:
            x = _ffn(x, *ff1, layer=l, final_norm=False)
        q, kt, v, fab, gate = _pre(x, row(g_mix[l]), w_in, w_gate, cos, sin, dft, layer=l,
                                   n_prompt=n_prompt, s_prompt=sp, s_sample=ss)
        lamv = jnp.stack([lam_q1[l], lam_k1[l], lam_q2[l], lam_k2[l]])
        o = _attn(q, kt, v, lamv, row(g_sub[l]), None, row0=0, batch=bp, seq=sp, lam_init=lam_init)
        o = _attn(q, kt, v, lamv, row(g_sub[l]), o, row0=n_prompt, batch=bs, seq=ss, lam_init=lam_init)
        f = _seqfft(fab, *seq_tabs[sp], None, row0=0, batch=bp, seq=sp)
        f = _seqfft(fab, *seq_tabs[ss], f, row0=n_prompt, batch=bs, seq=ss)
        x = _post(x, o, f, gate, w_pa, w_pf, w_o, layer=l)
        ff2 = (row(g_ff2[l]), w_ff2_up, w_ff2_down, gf)
        if l < DEPTH - 1:
            x = _ffn(x, *ff2, layer=l, final_norm=False)
    last = DEPTH - 1
    y_prompt = _ffn(x, *ff2, layer=last, final_norm=True, rows=n_prompt)
    y_sample = _ffn(x, *ff2, layer=last, final_norm=True, rows=n_sample, in_row0=n_prompt)
    return y_prompt.reshape(bp, sp, D_MODEL), y_sample.reshape(bs, ss, D_MODEL)
```

```python
import functools
import math

import jax
import jax.numpy as jnp
from jax import lax
from jax.experimental import pallas as pl
from jax.experimental.pallas import tpu as pltpu

D_MODEL = 1024
DEPTH = 4
N_HEADS = 4
HEAD_DIM = 64
V_DIM = 2 * HEAD_DIM
QK_W = N_HEADS * 2 * HEAD_DIM
ATTN_W = N_HEADS * V_DIM
N_FGROUPS = 4
FGROUP_DIM = 128
FOURIER_W = N_FGROUPS * FGROUP_DIM
IN_W = 2 * QK_W + ATTN_W + FOURIER_W
D_FF = 2816
ROPE_THETA = 10000.0
EPS = 1e-6

LANES = 128
QKV_W = 2 * QK_W + ATTN_W
Q_SCALE = HEAD_DIM ** -0.5 * math.log2(math.e)
VMEM_LIMIT = 56 * 1024 * 1024
TOKEN_TILE = 1024
FFN_TILE = 1024
ATTN_TILE = 512 * 4096
FFT_RADIX = 16
F32_ROWS = 8
PERM_ROWS = 256
ROW_RUN = 256

F32 = jnp.float32
BF16 = jnp.bfloat16


def _rms(x, g):
    return x * lax.rsqrt(jnp.mean(x * x, axis=-1, keepdims=True) + EPS) * g


def _dot(a, b):
    return jnp.dot(a, b, preferred_element_type=F32)


def _params(sem):
    return pltpu.CompilerParams(dimension_semantics=sem, vmem_limit_bytes=VMEM_LIMIT)


def _const_spec(shape):
    return pl.BlockSpec(shape, lambda *_: (0,) * len(shape), pipeline_mode=pl.Buffered(1))


def _layer_spec(shape, layer, col_block=0):
    return pl.BlockSpec((None,) + shape, lambda *_: (layer, 0, col_block), pipeline_mode=pl.Buffered(1))


def _ffn_kernel(x_ref, g_ref, wg_ref, wu_ref, wd_ref, gf_ref, *rest, final_norm):
    o_ref = rest[-1]
    for part in range(x_ref.shape[0] // ROW_RUN):
        rows = slice(part * ROW_RUN, (part + 1) * ROW_RUN)
        x = x_ref[rows, :]
        h = _rms(x, g_ref[...]).astype(BF16)
        gate = _dot(h, wg_ref[...])
        up = _dot(h, wu_ref[...])
        act = (gate / (1.0 + jnp.exp(-gate)) * up).astype(BF16)
        y = x + 0.5 * _dot(act, wd_ref[...])
        if final_norm:
            y = _rms(y, gf_ref[...])
        o_ref[rows, :] = y


def _ffn(x, g, w_up, w_down, g_final, *, layer, final_norm, rows=None, in_row0=0, out_rows=None, out_row0=0,
         out_prev=None):
    tm = FFN_TILE
    rows = x.shape[0] if rows is None else rows
    out_rows = rows if out_rows is None else out_rows
    in0, out0 = in_row0 // tm, out_row0 // tm
    in_specs = [
        pl.BlockSpec((tm, D_MODEL), lambda i: (in0 + i, 0)),
        _const_spec((1, D_MODEL)),
        _layer_spec((D_MODEL, D_FF), layer, 0),
        _layer_spec((D_MODEL, D_FF), layer, 1),
        _layer_spec((D_FF, D_MODEL), layer),
        _const_spec((1, D_MODEL)),
    ]
    args = [x, g, w_up, w_up, w_down, g_final]
    aliases = {}
    if out_prev is not None:
        in_specs.append(pl.BlockSpec(memory_space=pl.ANY))
        args.append(out_prev)
        aliases = {len(args) - 1: 0}
    return pl.pallas_call(
        functools.partial(_ffn_kernel, final_norm=final_norm),
        grid=(rows // tm,),
        in_specs=in_specs,
        out_specs=pl.BlockSpec((tm, D_MODEL), lambda i: (out0 + i, 0)),
        out_shape=jax.ShapeDtypeStruct((out_rows, D_MODEL), F32),
        input_output_aliases=aliases,
        compiler_params=_params(("parallel",)),
        name="ffn",
    )(*args)


def _pre_kernel(x_ref, g_ref, win_ref, wgate_ref, cos_ref, sin_ref, dft_ref, perm_ref,
                q_ref, kt_ref, v_ref, fab_ref, gate_ref):
    lane = lax.broadcasted_iota(jnp.int32, (PERM_ROWS, LANES), 1)
    low_half = (lane % HEAD_DIM) < (HEAD_DIM // 2)
    dft = dft_ref[...]
    span = PERM_ROWS // FFT_RADIX
    for part in range(x_ref.shape[0] // PERM_ROWS):
        rows = slice(part * PERM_ROWS, (part + 1) * PERM_ROWS)
        h = _rms(x_ref[rows, :], g_ref[...]).astype(BF16)
        proj = _dot(h, win_ref[...])
        cos = cos_ref[rows, :]
        sin = sin_ref[rows, :]
        for j in range(2 * N_HEADS):
            t = proj[:, j * LANES:(j + 1) * LANES]
            rot = jnp.where(low_half,
                            pltpu.roll(t, LANES - HEAD_DIM // 2, axis=1),
                            pltpu.roll(t, HEAD_DIM // 2, axis=1))
            roped = t * cos + rot * sin
            if j < N_HEADS:
                q_ref[rows, j * LANES:(j + 1) * LANES] = (roped * Q_SCALE).astype(BF16)
            else:
                kt_ref[(j - N_HEADS) * LANES:(j - N_HEADS + 1) * LANES, rows] = roped.T.astype(BF16)
        v_ref[rows, :] = proj[:, 2 * QK_W:QKV_W].astype(BF16)
        up = _dot(perm_ref[...], proj[:, QKV_W:].astype(BF16)).astype(BF16)
        dst = slice(part * span, (part + 1) * span)
        for grp in range(N_FGROUPS):
            ab = _dot(up[:, grp * FGROUP_DIM:(grp + 1) * FGROUP_DIM], dft)
            for n2 in range(FFT_RADIX):
                src = slice(n2 * span, (n2 + 1) * span)
                col = n2 * 2 * FOURIER_W + grp * FGROUP_DIM
                fab_ref[dst, col:col + FGROUP_DIM] = ab[src, :FGROUP_DIM].astype(BF16)
                fab_ref[dst, col + FOURIER_W:col + FOURIER_W + FGROUP_DIM] = ab[src, FGROUP_DIM:].astype(BF16)
        z = _dot(h, wgate_ref[...])
        gate_ref[rows, :] = (1.0 / (1.0 + jnp.exp(-z))).astype(BF16)


def _pre(x, g, w_in, w_gate, cos, sin, dft, *, layer, n_prompt, s_prompt, s_sample):
    t = x.shape[0]
    tm = TOKEN_TILE
    n_prompt_tiles = n_prompt // tm
    idx = jnp.arange(PERM_ROWS)
    src = FFT_RADIX * (idx % (PERM_ROWS // FFT_RADIX)) + idx // (PERM_ROWS // FFT_RADIX)
    perm = (src[:, None] == idx[None, :]).astype(BF16)

    def pos_map(i):
        p = jnp.where(i < n_prompt_tiles, i % (s_prompt // tm), (i - n_prompt_tiles) % (s_sample // tm))
        return (p, 0)

    return pl.pallas_call(
        _pre_kernel,
        grid=(t // tm,),
        in_specs=[
            pl.BlockSpec((tm, D_MODEL), lambda i: (i, 0)),
            _const_spec((1, D_MODEL)),
            _layer_spec((D_MODEL, IN_W), layer),
            _layer_spec((D_MODEL, 2 * D_MODEL), layer),
            pl.BlockSpec((tm, LANES), pos_map),
            pl.BlockSpec((tm, LANES), pos_map),
            _const_spec((FGROUP_DIM, 2 * FGROUP_DIM)),
            _const_spec((PERM_ROWS, PERM_ROWS)),
        ],
        out_specs=[
            pl.BlockSpec((tm, QK_W), lambda i: (i, 0)),
            pl.BlockSpec((QK_W, tm), lambda i: (0, i)),
            pl.BlockSpec((tm, ATTN_W), lambda i: (i, 0)),
            pl.BlockSpec((tm // FFT_RADIX, FFT_RADIX * 2 * FOURIER_W), lambda i: (i, 0)),
            pl.BlockSpec((tm, 2 * D_MODEL), lambda i: (i, 0)),
        ],
        out_shape=[
            jax.ShapeDtypeStruct((t, QK_W), BF16),
            jax.ShapeDtypeStruct((QK_W, t), BF16),
            jax.ShapeDtypeStruct((t, ATTN_W), BF16),
            jax.ShapeDtypeStruct((t // FFT_RADIX, FFT_RADIX * 2 * FOURIER_W), BF16),
            jax.ShapeDtypeStruct((t, 2 * D_MODEL), BF16),
        ],
        compiler_params=_params(("parallel",)),
        name="mixer_in",
    )(x, g, w_in, w_gate, cos, sin, dft, perm)


def _attn_kernel(lamv_ref, q_ref, kt_ref, v_ref, gsub_ref, *rest, lam_init, n_items):
    o_ref = rest[-13]
    scr = rest[-12:]
    slots = tuple((scr[2 * k:2 * k + 2], scr[4 + 2 * k:6 + 2 * k], scr[8 + 2 * k:10 + 2 * k]) for k in range(2))
    t = pl.program_id(0)

    def scores(slot):
        s_refs, m_refs, _ = slots[slot]
        q = q_ref[...]
        kt = kt_ref[...]
        lane = lax.broadcasted_iota(jnp.int32, q.shape, 1)
        zero = jnp.zeros_like(q)
        for comp, qc in enumerate((jnp.where(lane < HEAD_DIM, q, zero), jnp.where(lane >= HEAD_DIM, q, zero))):
            s = _dot(qc, kt)
            s_refs[comp][...] = s
            m_refs[comp][...] = jnp.max(s, axis=-1, keepdims=True)

    def values(slot):
        s_refs, m_refs, r_refs = slots[slot]
        v = v_ref[...]
        v1 = jnp.concatenate([v, jnp.ones_like(v)], axis=1)
        for comp in range(2):
            r_refs[comp][...] = _dot(jnp.exp2(s_refs[comp][...] - m_refs[comp][...]).astype(BF16), v1)

    def output(slot):
        r0, r1 = (r[...] for r in slots[slot][2])
        lamv = lamv_ref[...]
        lam = (jnp.exp(jnp.sum(lamv[0:1] * lamv[1:2], axis=-1, keepdims=True))
               - jnp.exp(jnp.sum(lamv[2:3] * lamv[3:4], axis=-1, keepdims=True)) + lam_init)
        o = r0[:, :V_DIM] / r0[:, V_DIM:V_DIM + 1] - lam * (r1[:, :V_DIM] / r1[:, V_DIM:V_DIM + 1])
        o_ref[...] = (_rms(o, gsub_ref[...]) * (1.0 - lam_init)).astype(BF16)

    def stages(parity, do_scores, do_values, do_output):
        if do_output:
            output(parity)
        if do_scores:
            scores(parity)
        if do_values:
            values(1 - parity)

    @pl.when(t == 0)
    def _():
        stages(0, True, False, False)

    @pl.when(t == 1)
    def _():
        stages(1, True, True, False)

    for parity in range(2):
        @pl.when(jnp.logical_and(jnp.logical_and(t >= 2, t < n_items), t % 2 == parity))
        def _():
            stages(parity, True, True, True)

    @pl.when(t == n_items)
    def _():
        stages(n_items % 2, False, True, True)

    @pl.when(t == n_items + 1)
    def _():
        stages((n_items + 1) % 2, False, False, True)


def _attn(q, kt, v, lamv, gsub, o_prev, *, row0, batch, seq, lam_init):
    t = q.shape[0]
    tq = ATTN_TILE // seq
    b0 = row0 // seq
    q0 = row0 // tq
    nq = seq // tq
    n_items = batch * N_HEADS * nq

    def item(step):
        step = jnp.clip(step, 0, n_items - 1)
        return step // (N_HEADS * nq), (step // nq) % N_HEADS, step % nq

    def q_map(step):
        b, h, i = item(step)
        return (q0 + b * nq + i, h)

    def kt_map(step):
        b, h, _ = item(step)
        return (h, b0 + b)

    def v_map(step):
        b, h, _ = item(step - 1)
        return (b0 + b, h)

    def o_map(step):
        b, h, i = item(step - 2)
        return (q0 + b * nq + i, h)

    in_specs = [
        _const_spec((4, HEAD_DIM)),
        pl.BlockSpec((tq, LANES), q_map),
        pl.BlockSpec((LANES, seq), kt_map),
        pl.BlockSpec((seq, V_DIM), v_map),
        _const_spec((1, V_DIM)),
    ]
    args = [lamv, q, kt, v, gsub]
    aliases = {}
    if o_prev is not None:
        in_specs.append(pl.BlockSpec(memory_space=pl.ANY))
        args.append(o_prev)
        aliases = {len(args) - 1: 0}
    return pl.pallas_call(
        functools.partial(_attn_kernel, lam_init=lam_init, n_items=n_items),
        grid=(n_items + 2,),
        in_specs=in_specs,
        out_specs=pl.BlockSpec((tq, V_DIM), o_map),
        out_shape=jax.ShapeDtypeStruct((t, ATTN_W), BF16),
        scratch_shapes=([pltpu.VMEM((tq, seq), F32)] * 4 + [pltpu.VMEM((tq, 1), F32)] * 4
                        + [pltpu.VMEM((tq, 2 * V_DIM), F32)] * 4),
        input_output_aliases=aliases,
        compiler_params=_params(("arbitrary",)),
        name=f"diff_attn_s{seq}",
    )(*args)


def _cmul_const(x, w):
    re, im = x
    wr, wi = round(w.real, 12), round(w.imag, 12)
    if (wr, wi) == (1.0, 0.0):
        return re, im
    if (wr, wi) == (-1.0, 0.0):
        return -re, -im
    if (wr, wi) == (0.0, -1.0):
        return im, -re
    if (wr, wi) == (0.0, 1.0):
        return -im, re
    return re * wr - im * wi, re * wi + im * wr


def _fft(xs):
    n = len(xs)
    if n == 1:
        return xs
    even, odd = _fft(xs[0::2]), _fft(xs[1::2])
    out = [None] * n
    for k in range(n // 2):
        tr, ti = _cmul_const(odd[k], complex(math.cos(2 * math.pi * k / n), -math.sin(2 * math.pi * k / n)))
        er, ei = even[k]
        out[k] = (er + tr, ei + ti)
        out[k + n // 2] = (er - tr, ei - ti)
    return out


def _seqfft_kernel(m1_ref, twc_ref, tws_ref, fab_ref, *rest, scale):
    o_ref, wre, wim = rest[-3:]
    n = m1_ref.shape[1]
    for n2 in range(FFT_RADIX):
        ab = fab_ref[:, n2 * 2 * FOURIER_W:(n2 + 1) * 2 * FOURIER_W]
        p = _dot(m1_ref[...], ab)
        t_re = p[:n, :FOURIER_W] - p[n:, FOURIER_W:]
        t_im = -(p[:n, FOURIER_W:] + p[n:, :FOURIER_W])
        c = jnp.concatenate([twc_ref[n2]] * (FOURIER_W // LANES), axis=1)
        s = jnp.concatenate([tws_ref[n2]] * (FOURIER_W // LANES), axis=1)
        wre[n2] = t_re * c + t_im * s
        wim[n2] = t_im * c - t_re * s
    for g in range(n // F32_ROWS):
        rows = slice(g * F32_ROWS, (g + 1) * F32_ROWS)
        for lb in range(FOURIER_W // LANES):
            cols = slice(lb * LANES, (lb + 1) * LANES)
            ys = _fft([(wre[n2, rows, cols], wim[n2, rows, cols]) for n2 in range(FFT_RADIX)])
            for k2 in range(FFT_RADIX):
                o_ref[k2 * n + g * F32_ROWS:k2 * n + (g + 1) * F32_ROWS, cols] = (ys[k2][0] * scale).astype(BF16)


def _fft_tables(seq):
    n = seq // FFT_RADIX
    cr, sr = _dft_tables(n, n, n)
    twc, tws = _dft_tables(FFT_RADIX, n, seq)
    lanes = lambda a: jnp.broadcast_to(a[:, :, None], (FFT_RADIX, n, LANES))
    return jnp.concatenate([cr, sr], axis=0).astype(BF16), lanes(twc), lanes(tws)


def _seqfft(fab, m1, twc, tws, f_prev, *, row0, batch, seq):
    t = fab.shape[0] * FFT_RADIX
    n = seq // FFT_RADIX
    b0 = row0 // seq
    in_specs = [
        _const_spec((2 * n, n)),
        _const_spec((FFT_RADIX, n, LANES)),
        _const_spec((FFT_RADIX, n, LANES)),
        pl.BlockSpec((n, FFT_RADIX * 2 * FOURIER_W), lambda b: (b0 + b, 0)),
    ]
    args = [m1, twc, tws, fab]
    aliases = {}
    if f_prev is not None:
        in_specs.append(pl.BlockSpec(memory_space=pl.ANY))
        args.append(f_prev)
        aliases = {len(args) - 1: 0}
    return pl.pallas_call(
        functools.partial(_seqfft_kernel, scale=1.0 / math.sqrt(seq * FGROUP_DIM)),
        grid=(batch,),
        in_specs=in_specs,
        out_specs=pl.BlockSpec((seq, FOURIER_W), lambda b: (b0 + b, 0)),
        out_shape=jax.ShapeDtypeStruct((t, FOURIER_W), BF16),
        scratch_shapes=[pltpu.VMEM((FFT_RADIX, n, FOURIER_W), F32)] * 2,
        input_output_aliases=aliases,
        compiler_params=_params(("parallel",)),
        name=f"seq_fft_s{seq}",
    )(*args)


def _post_kernel(x_ref, o_ref, f_ref, gate_ref, wpa_ref, wpf_ref, wo_ref, out_ref):
    br_a = _dot(o_ref[...], wpa_ref[...])
    br_f = _dot(f_ref[...], wpf_ref[...])
    merged = (gate_ref[:, :D_MODEL].astype(F32) * br_a + gate_ref[:, D_MODEL:].astype(F32) * br_f)
    out_ref[...] = x_ref[...] + _dot(merged.astype(BF16), wo_ref[...])


def _post(x, o, f, gate, w_pa, w_pf, w_o, *, layer):
    t = x.shape[0]
    tm = TOKEN_TILE
    return pl.pallas_call(
        _post_kernel,
        grid=(t // tm,),
        in_specs=[
            pl.BlockSpec((tm, D_MODEL), lambda i: (i, 0)),
            pl.BlockSpec((tm, ATTN_W), lambda i: (i, 0)),
            pl.BlockSpec((tm, FOURIER_W), lambda i: (i, 0)),
            pl.BlockSpec((tm, 2 * D_MODEL), lambda i: (i, 0)),
            _layer_spec((ATTN_W, D_MODEL), layer),
            _layer_spec((FOURIER_W, D_MODEL), layer),
            _layer_spec((D_MODEL, D_MODEL), layer),
        ],
        out_specs=pl.BlockSpec((tm, D_MODEL), lambda i: (i, 0)),
        out_shape=jax.ShapeDtypeStruct((t, D_MODEL), F32),
        compiler_params=_params(("parallel",)),
        name="mixer_out",
    )(x, o, f, gate, w_pa, w_pf, w_o)


def _rope_tables(seq):
    inv = 1.0 / (ROPE_THETA ** (jnp.arange(0, HEAD_DIM, 2, dtype=F32) / HEAD_DIM))
    ang = jnp.arange(seq, dtype=F32)[:, None] * inv[None, :]
    ang = jnp.concatenate([ang, ang, ang, ang], axis=-1)
    lane = jnp.arange(LANES)
    sign = jnp.where((lane % HEAD_DIM) < HEAD_DIM // 2, -1.0, 1.0).astype(F32)
    return jnp.cos(ang), jnp.sin(ang) * sign[None, :]


def _dft_tables(rows, cols, n):
    j = jnp.arange(rows, dtype=jnp.int32)
    k = jnp.arange(cols, dtype=jnp.int32)
    ang = ((j[:, None] * k[None, :]) % n).astype(F32) * (2.0 * math.pi / n)
    return jnp.cos(ang), jnp.sin(ang)


def kernel(x_prompt, x_sample, g_ff1, w_ff1_up, w_ff1_down, g_mix, w_in, lam_q1, lam_k1, lam_q2, lam_k2,
           g_sub, w_pa, w_pf, w_gate, w_o, g_ff2, w_ff2_up, w_ff2_down, g_final):
    bp, sp, _ = x_prompt.shape
    bs, ss, _ = x_sample.shape
    n_prompt = bp * sp
    assert n_prompt % ss == 0 and sp <= ss
    n_sample = bs * ss
    n_tok = n_prompt + n_sample

    cos, sin = _rope_tables(ss)
    cg, sg = _dft_tables(FGROUP_DIM, FGROUP_DIM, FGROUP_DIM)
    dft = jnp.concatenate([cg, sg], axis=1).astype(BF16)
    seq_tabs = {s: _fft_tables(s) for s in {sp, ss}}

    row = lambda v: v.reshape(1, -1)
    gf = row(g_final)
    w_ff1_up, w_ff1_down, w_ff2_up, w_ff2_down, w_in, w_gate, w_pa, w_pf, w_o = (
        w.astype(BF16) for w in (w_ff1_up, w_ff1_down, w_ff2_up, w_ff2_down, w_in, w_gate, w_pa, w_pf, w_o))
    x = None
    for l in range(DEPTH):
        lam_init = 0.8 - 0.6 * math.exp(-0.3 * l)
        ff1 = (row(g_ff1[l]), w_ff1_up, w_ff1_down, gf)
        if l == 0:
            x = _ffn(x_prompt.reshape(n_prompt, D_MODEL), *ff1, layer=l, final_norm=False, out_rows=n_tok)
            x = _ffn(x_sample.reshape(n_sample, D_MODEL), *ff1, layer=l, final_norm=False, out_rows=n_tok,
                     out_row0=n_prompt, out_prev=x)
        else:
            x = _ffn(x, *ff1, layer=l, final_norm=False)
        q, kt, v, fab, gate = _pre(x, row(g_mix[l]), w_in, w_gate, cos, sin, dft, layer=l,
                                   n_prompt=n_prompt, s_prompt=sp, s_sample=ss)
        lamv = jnp.stack([lam_q1[l], lam_k1[l], lam_q2[l], lam_k2[l]])
        o = _attn(q, kt, v, lamv, row(g_sub[l]), None, row0=0, batch=bp, seq=sp, lam_init=lam_init)
        o = _attn(q, kt, v, lamv, row(g_sub[l]), o, row0=n_prompt, batch=bs, seq=ss, lam_init=lam_init)
        f = _seqfft(fab, *seq_tabs[sp], None, row0=0, batch=bp, seq=sp)
        f = _seqfft(fab, *seq_tabs[ss], f, row0=n_prompt, batch=bs, seq=ss)
        x = _post(x, o, f, gate, w_pa, w_pf, w_o, layer=l)
        ff2 = (row(g_ff2[l]), w_ff2_up, w_ff2_down, gf)
        if l < DEPTH - 1:
            x = _ffn(x, *ff2, layer=l, final_norm=False)
    last = DEPTH - 1
    y_prompt = _ffn(x, *ff2, layer=last, final_norm=True, rows=n_prompt)
    y_sample = _ffn(x, *ff2, layer=last, final_norm=True, rows=n_sample, in_row0=n_prompt)
    return y_prompt.reshape(bp, sp, D_MODEL), y_sample.reshape(bs, ss, D_MODEL)
```

```python
import functools
import math

import jax
import jax.numpy as jnp
from jax import lax
from jax.experimental import pallas as pl
from jax.experimental.pallas import tpu as pltpu

D_MODEL = 1024
DEPTH = 4
N_HEADS = 4
HEAD_DIM = 64
V_DIM = 2 * HEAD_DIM
QK_W = N_HEADS * 2 * HEAD_DIM
ATTN_W = N_HEADS * V_DIM
N_FGROUPS = 4
FGROUP_DIM = 128
FOURIER_W = N_FGROUPS * FGROUP_DIM
IN_W = 2 * QK_W + ATTN_W + FOURIER_W
D_FF = 2816
ROPE_THETA = 10000.0
EPS = 1e-6

LANES = 128
QKV_W = 2 * QK_W + ATTN_W
Q_SCALE = HEAD_DIM ** -0.5 * math.log2(math.e)
VMEM_LIMIT = 56 * 1024 * 1024
TOKEN_TILE = 1024
FFN_TILE = 1024
ATTN_TILE = 512 * 4096
FFT_RADIX = 16
F32_ROWS = 8
PERM_ROWS = 256
ROW_RUN = 512

F32 = jnp.float32
BF16 = jnp.bfloat16


def _rms(x, g):
    return x * lax.rsqrt(jnp.mean(x * x, axis=-1, keepdims=True) + EPS) * g


def _dot(a, b):
    return jnp.dot(a, b, preferred_element_type=F32)


def _params(sem):
    return pltpu.CompilerParams(dimension_semantics=sem, vmem_limit_bytes=VMEM_LIMIT)


def _const_spec(shape):
    return pl.BlockSpec(shape, lambda *_: (0,) * len(shape), pipeline_mode=pl.Buffered(1))


def _layer_spec(shape, layer, col_block=0):
    return pl.BlockSpec((None,) + shape, lambda *_: (layer, 0, col_block), pipeline_mode=pl.Buffered(1))


def _ffn_kernel(x_ref, g_ref, wg_ref, wu_ref, wd_ref, gf_ref, *rest, final_norm):
    o_ref = rest[-1]
    for part in range(x_ref.shape[0] // ROW_RUN):
        rows = slice(part * ROW_RUN, (part + 1) * ROW_RUN)
        x = x_ref[rows, :]
        h = _rms(x, g_ref[...]).astype(BF16)
        gate = _dot(h, wg_ref[...])
        up = _dot(h, wu_ref[...])
        act = (gate / (1.0 + jnp.exp(-gate)) * up).astype(BF16)
        y = x + 0.5 * _dot(act, wd_ref[...])
        if final_norm:
            y = _rms(y, gf_ref[...])
        o_ref[rows, :] = y


def _ffn(x, g, w_up, w_down, g_final, *, layer, final_norm, rows=None, in_row0=0, out_rows=None, out_row0=0,
         out_prev=None):
    tm = FFN_TILE
    rows = x.shape[0] if rows is None else rows
    out_rows = rows if out_rows is None else out_rows
    in0, out0 = in_row0 // tm, out_row0 // tm
    in_specs = [
        pl.BlockSpec((tm, D_MODEL), lambda i: (in0 + i, 0)),
        _const_spec((1, D_MODEL)),
        _layer_spec((D_MODEL, D_FF), layer, 0),
        _layer_spec((D_MODEL, D_FF), layer, 1),
        _layer_spec((D_FF, D_MODEL), layer),
        _const_spec((1, D_MODEL)),
    ]
    args = [x, g, w_up, w_up, w_down, g_final]
    aliases = {}
    if out_prev is not None:
        in_specs.append(pl.BlockSpec(memory_space=pl.ANY))
        args.append(out_prev)
        aliases = {len(args) - 1: 0}
    return pl.pallas_call(
        functools.partial(_ffn_kernel, final_norm=final_norm),
        grid=(rows // tm,),
        in_specs=in_specs,
        out_specs=pl.BlockSpec((tm, D_MODEL), lambda i: (out0 + i, 0)),
        out_shape=jax.ShapeDtypeStruct((out_rows, D_MODEL), F32),
        input_output_aliases=aliases,
        compiler_params=_params(("parallel",)),
        name="ffn",
    )(*args)


def _pre_kernel(x_ref, g_ref, win_ref, wgate_ref, cos_ref, sin_ref, dft_ref, perm_ref,
                q_ref, kt_ref, v_ref, fab_ref, gate_ref):
    lane = lax.broadcasted_iota(jnp.int32, (PERM_ROWS, LANES), 1)
    low_half = (lane % HEAD_DIM) < (HEAD_DIM // 2)
    dft = dft_ref[...]
    span = PERM_ROWS // FFT_RADIX
    for part in range(x_ref.shape[0] // PERM_ROWS):
        rows = slice(part * PERM_ROWS, (part + 1) * PERM_ROWS)
        h = _rms(x_ref[rows, :], g_ref[...]).astype(BF16)
        proj = _dot(h, win_ref[...])
        cos = cos_ref[rows, :]
        sin = sin_ref[rows, :]
        for j in range(2 * N_HEADS):
            t = proj[:, j * LANES:(j + 1) * LANES]
            rot = jnp.where(low_half,
                            pltpu.roll(t, LANES - HEAD_DIM // 2, axis=1),
                            pltpu.roll(t, HEAD_DIM // 2, axis=1))
            roped = t * cos + rot * sin
            if j < N_HEADS:
                q_ref[rows, j * LANES:(j + 1) * LANES] = (roped * Q_SCALE).astype(BF16)
            else:
                kt_ref[(j - N_HEADS) * LANES:(j - N_HEADS + 1) * LANES, rows] = roped.T.astype(BF16)
        v_ref[rows, :] = proj[:, 2 * QK_W:QKV_W].astype(BF16)
        up = _dot(perm_ref[...], proj[:, QKV_W:].astype(BF16)).astype(BF16)
        dst = slice(part * span, (part + 1) * span)
        for grp in range(N_FGROUPS):
            ab = _dot(up[:, grp * FGROUP_DIM:(grp + 1) * FGROUP_DIM], dft)
            for n2 in range(FFT_RADIX):
                src = slice(n2 * span, (n2 + 1) * span)
                col = n2 * 2 * FOURIER_W + grp * FGROUP_DIM
                fab_ref[dst, col:col + FGROUP_DIM] = ab[src, :FGROUP_DIM].astype(BF16)
                fab_ref[dst, col + FOURIER_W:col + FOURIER_W + FGROUP_DIM] = ab[src, FGROUP_DIM:].astype(BF16)
        z = _dot(h, wgate_ref[...])
        gate_ref[rows, :] = (1.0 / (1.0 + jnp.exp(-z))).astype(BF16)


def _pre(x, g, w_in, w_gate, cos, sin, dft, *, layer, n_prompt, s_prompt, s_sample):
    t = x.shape[0]
    tm = TOKEN_TILE
    n_prompt_tiles = n_prompt // tm
    idx = jnp.arange(PERM_ROWS)
    src = FFT_RADIX * (idx % (PERM_ROWS // FFT_RADIX)) + idx // (PERM_ROWS // FFT_RADIX)
    perm = (src[:, None] == idx[None, :]).astype(BF16)

    def pos_map(i):
        p = jnp.where(i < n_prompt_tiles, i % (s_prompt // tm), (i - n_prompt_tiles) % (s_sample // tm))
        return (p, 0)

    return pl.pallas_call(
        _pre_kernel,
        grid=(t // tm,),
        in_specs=[
            pl.BlockSpec((tm, D_MODEL), lambda i: (i, 0)),
            _const_spec((1, D_MODEL)),
            _layer_spec((D_MODEL, IN_W), layer),
            _layer_spec((D_MODEL, 2 * D_MODEL), layer),
            pl.BlockSpec((tm, LANES), pos_map),
            pl.BlockSpec((tm, LANES), pos_map),
            _const_spec((FGROUP_DIM, 2 * FGROUP_DIM)),
            _const_spec((PERM_ROWS, PERM_ROWS)),
        ],
        out_specs=[
            pl.BlockSpec((tm, QK_W), lambda i: (i, 0)),
            pl.BlockSpec((QK_W, tm), lambda i: (0, i)),
            pl.BlockSpec((tm, ATTN_W), lambda i: (i, 0)),
            pl.BlockSpec((tm // FFT_RADIX, FFT_RADIX * 2 * FOURIER_W), lambda i: (i, 0)),
            pl.BlockSpec((tm, 2 * D_MODEL), lambda i: (i, 0)),
        ],
        out_shape=[
            jax.ShapeDtypeStruct((t, QK_W), BF16),
            jax.ShapeDtypeStruct((QK_W, t), BF16),
            jax.ShapeDtypeStruct((t, ATTN_W), BF16),
            jax.ShapeDtypeStruct((t // FFT_RADIX, FFT_RADIX * 2 * FOURIER_W), BF16),
            jax.ShapeDtypeStruct((t, 2 * D_MODEL), BF16),
        ],
        compiler_params=_params(("parallel",)),
        name="mixer_in",
    )(x, g, w_in, w_gate, cos, sin, dft, perm)


def _attn_kernel(lamv_ref, q_ref, kt_ref, v_ref, gsub_ref, *rest, lam_init, n_items):
    o_ref = rest[-13]
    scr = rest[-12:]
    slots = tuple((scr[2 * k:2 * k + 2], scr[4 + 2 * k:6 + 2 * k], scr[8 + 2 * k:10 + 2 * k]) for k in range(2))
    t = pl.program_id(0)

    def scores(slot):
        s_refs, m_refs, _ = slots[slot]
        q = q_ref[...]
        kt = kt_ref[...]
        lane = lax.broadcasted_iota(jnp.int32, q.shape, 1)
        zero = jnp.zeros_like(q)
        for comp, qc in enumerate((jnp.where(lane < HEAD_DIM, q, zero), jnp.where(lane >= HEAD_DIM, q, zero))):
            s = _dot(qc, kt)
            s_refs[comp][...] = s
            m_refs[comp][...] = jnp.max(s, axis=-1, keepdims=True)

    def values(slot):
        s_refs, m_refs, r_refs = slots[slot]
        v = v_ref[...]
        v1 = jnp.concatenate([v, jnp.ones_like(v)], axis=1)
        for comp in range(2):
            r_refs[comp][...] = _dot(jnp.exp2(s_refs[comp][...] - m_refs[comp][...]).astype(BF16), v1)

    def output(slot):
        r0, r1 = (r[...] for r in slots[slot][2])
        lamv = lamv_ref[...]
        lam = (jnp.exp(jnp.sum(lamv[0:1] * lamv[1:2], axis=-1, keepdims=True))
               - jnp.exp(jnp.sum(lamv[2:3] * lamv[3:4], axis=-1, keepdims=True)) + lam_init)
        o = r0[:, :V_DIM] / r0[:, V_DIM:V_DIM + 1] - lam * (r1[:, :V_DIM] / r1[:, V_DIM:V_DIM + 1])
        o_ref[...] = (_rms(o, gsub_ref[...]) * (1.0 - lam_init)).astype(BF16)

    def stages(parity, do_scores, do_values, do_output):
        if do_output:
            output(parity)
        if do_scores:
            scores(parity)
        if do_values:
            values(1 - parity)

    @pl.when(t == 0)
    def _():
        stages(0, True, False, False)

    @pl.when(t == 1)
    def _():
        stages(1, True, True, False)

    for parity in range(2):
        @pl.when(jnp.logical_and(jnp.logical_and(t >= 2, t < n_items), t % 2 == parity))
        def _():
            stages(parity, True, True, True)

    @pl.when(t == n_items)
    def _():
        stages(n_items % 2, False, True, True)

    @pl.when(t == n_items + 1)
    def _():
        stages((n_items + 1) % 2, False, False, True)


def _attn(q, kt, v, lamv, gsub, o_prev, *, row0, batch, seq, lam_init):
    t = q.shape[0]
    tq = ATTN_TILE // seq
    b0 = row0 // seq
    q0 = row0 // tq
    nq = seq // tq
    n_items = batch * N_HEADS * nq

    def item(step):
        step = jnp.clip(step, 0, n_items - 1)
        return step // (N_HEADS * nq), (step // nq) % N_HEADS, step % nq

    def q_map(step):
        b, h, i = item(step)
        return (q0 + b * nq + i, h)

    def kt_map(step):
        b, h, _ = item(step)
        return (h, b0 + b)

    def v_map(step):
        b, h, _ = item(step - 1)
        return (b0 + b, h)

    def o_map(step):
        b, h, i = item(step - 2)
        return (q0 + b * nq + i, h)

    in_specs = [
        _const_spec((4, HEAD_DIM)),
        pl.BlockSpec((tq, LANES), q_map),
        pl.BlockSpec((LANES, seq), kt_map),
        pl.BlockSpec((seq, V_DIM), v_map),
        _const_spec((1, V_DIM)),
    ]
    args = [lamv, q, kt, v, gsub]
    aliases = {}
    if o_prev is not None:
        in_specs.append(pl.BlockSpec(memory_space=pl.ANY))
        args.append(o_prev)
        aliases = {len(args) - 1: 0}
    return pl.pallas_call(
        functools.partial(_attn_kernel, lam_init=lam_init, n_items=n_items),
        grid=(n_items + 2,),
        in_specs=in_specs,
        out_specs=pl.BlockSpec((tq, V_DIM), o_map),
        out_shape=jax.ShapeDtypeStruct((t, ATTN_W), BF16),
        scratch_shapes=([pltpu.VMEM((tq, seq), F32)] * 4 + [pltpu.VMEM((tq, 1), F32)] * 4
                        + [pltpu.VMEM((tq, 2 * V_DIM), F32)] * 4),
        input_output_aliases=aliases,
        compiler_params=_params(("arbitrary",)),
        name=f"diff_attn_s{seq}",
    )(*args)


def _cmul_const(x, w):
    re, im = x
    wr, wi = round(w.real, 12), round(w.imag, 12)
    if (wr, wi) == (1.0, 0.0):
        return re, im
    if (wr, wi) == (-1.0, 0.0):
        return -re, -im
    if (wr, wi) == (0.0, -1.0):
        return im, -re
    if (wr, wi) == (0.0, 1.0):
        return -im, re
    return re * wr - im * wi, re * wi + im * wr


def _fft(xs):
    n = len(xs)
    if n == 1:
        return xs
    even, odd = _fft(xs[0::2]), _fft(xs[1::2])
    out = [None] * n
    for k in range(n // 2):
        tr, ti = _cmul_const(odd[k], complex(math.cos(2 * math.pi * k / n), -math.sin(2 * math.pi * k / n)))
        er, ei = even[k]
        out[k] = (er + tr, ei + ti)
        out[k + n // 2] = (er - tr, ei - ti)
    return out


def _seqfft_kernel(m1_ref, twc_ref, tws_ref, fab_ref, *rest, scale):
    o_ref, wre, wim = rest[-3:]
    n = m1_ref.shape[1]
    for n2 in range(FFT_RADIX):
        ab = fab_ref[:, n2 * 2 * FOURIER_W:(n2 + 1) * 2 * FOURIER_W]
        p = _dot(m1_ref[...], ab)
        t_re = p[:n, :FOURIER_W] - p[n:, FOURIER_W:]
        t_im = -(p[:n, FOURIER_W:] + p[n:, :FOURIER_W])
        c = jnp.concatenate([twc_ref[n2]] * (FOURIER_W // LANES), axis=1)
        s = jnp.concatenate([tws_ref[n2]] * (FOURIER_W // LANES), axis=1)
        wre[n2] = t_re * c + t_im * s
        wim[n2] = t_im * c - t_re * s
    for g in range(n // F32_ROWS):
        rows = slice(g * F32_ROWS, (g + 1) * F32_ROWS)
        for lb in range(FOURIER_W // LANES):
            cols = slice(lb * LANES, (lb + 1) * LANES)
            ys = _fft([(wre[n2, rows, cols], wim[n2, rows, cols]) for n2 in range(FFT_RADIX)])
            for k2 in range(FFT_RADIX):
                o_ref[k2 * n + g * F32_ROWS:k2 * n + (g + 1) * F32_ROWS, cols] = (ys[k2][0] * scale).astype(BF16)


def _fft_tables(seq):
    n = seq // FFT_RADIX
    cr, sr = _dft_tables(n, n, n)
    twc, tws = _dft_tables(FFT_RADIX, n, seq)
    lanes = lambda a: jnp.broadcast_to(a[:, :, None], (FFT_RADIX, n, LANES))
    return jnp.concatenate([cr, sr], axis=0).astype(BF16), lanes(twc), lanes(tws)


def _seqfft(fab, m1, twc, tws, f_prev, *, row0, batch, seq):
    t = fab.shape[0] * FFT_RADIX
    n = seq // FFT_RADIX
    b0 = row0 // seq
    in_specs = [
        _const_spec((2 * n, n)),
        _const_spec((FFT_RADIX, n, LANES)),
        _const_spec((FFT_RADIX, n, LANES)),
        pl.BlockSpec((n, FFT_RADIX * 2 * FOURIER_W), lambda b: (b0 + b, 0)),
    ]
    args = [m1, twc, tws, fab]
    aliases = {}
    if f_prev is not None:
        in_specs.append(pl.BlockSpec(memory_space=pl.ANY))
        args.append(f_prev)
        aliases = {len(args) - 1: 0}
    return pl.pallas_call(
        functools.partial(_seqfft_kernel, scale=1.0 / math.sqrt(seq * FGROUP_DIM)),
        grid=(batch,),
        in_specs=in_specs,
        out_specs=pl.BlockSpec((seq, FOURIER_W), lambda b: (b0 + b, 0)),
        out_shape=jax.ShapeDtypeStruct((t, FOURIER_W), BF16),
        scratch_shapes=[pltpu.VMEM((FFT_RADIX, n, FOURIER_W), F32)] * 2,
        input_output_aliases=aliases,
        compiler_params=_params(("parallel",)),
        name=f"seq_fft_s{seq}",
    )(*args)


def _post_kernel(x_ref, o_ref, f_ref, gate_ref, wpa_ref, wpf_ref, wo_ref, out_ref):
    br_a = _dot(o_ref[...], wpa_ref[...])
    br_f = _dot(f_ref[...], wpf_ref[...])
    merged = (gate_ref[:, :D_MODEL].astype(F32) * br_a + gate_ref[:, D_MODEL:].astype(F32) * br_f)
    out_ref[...] = x_ref[...] + _dot(merged.astype(BF16), wo_ref[...])


def _post(x, o, f, gate, w_pa, w_pf, w_o, *, layer):
    t = x.shape[0]
    tm = TOKEN_TILE
    return pl.pallas_call(
        _post_kernel,
        grid=(t // tm,),
        in_specs=[
            pl.BlockSpec((tm, D_MODEL), lambda i: (i, 0)),
            pl.BlockSpec((tm, ATTN_W), lambda i: (i, 0)),
            pl.BlockSpec((tm, FOURIER_W), lambda i: (i, 0)),
            pl.BlockSpec((tm, 2 * D_MODEL), lambda i: (i, 0)),
            _layer_spec((ATTN_W, D_MODEL), layer),
            _layer_spec((FOURIER_W, D_MODEL), layer),
            _layer_spec((D_MODEL, D_MODEL), layer),
        ],
        out_specs=pl.BlockSpec((tm, D_MODEL), lambda i: (i, 0)),
        out_shape=jax.ShapeDtypeStruct((t, D_MODEL), F32),
        compiler_params=_params(("parallel",)),
        name="mixer_out",
    )(x, o, f, gate, w_pa, w_pf, w_o)


def _rope_tables(seq):
    inv = 1.0 / (ROPE_THETA ** (jnp.arange(0, HEAD_DIM, 2, dtype=F32) / HEAD_DIM))
    ang = jnp.arange(seq, dtype=F32)[:, None] * inv[None, :]
    ang = jnp.concatenate([ang, ang, ang, ang], axis=-1)
    lane = jnp.arange(LANES)
    sign = jnp.where((lane % HEAD_DIM) < HEAD_DIM // 2, -1.0, 1.0).astype(F32)
    return jnp.cos(ang), jnp.sin(ang) * sign[None, :]


def _dft_tables(rows, cols, n):
    j = jnp.arange(rows, dtype=jnp.int32)
    k = jnp.arange(cols, dtype=jnp.int32)
    ang = ((j[:, None] * k[None, :]) % n).astype(F32) * (2.0 * math.pi / n)
    return jnp.cos(ang), jnp.sin(ang)


def kernel(x_prompt, x_sample, g_ff1, w_ff1_up, w_ff1_down, g_mix, w_in, lam_q1, lam_k1, lam_q2, lam_k2,
           g_sub, w_pa, w_pf, w_gate, w_o, g_ff2, w_ff2_up, w_ff2_down, g_final):
    bp, sp, _ = x_prompt.shape
    bs, ss, _ = x_sample.shape
    n_prompt = bp * sp
    assert n_prompt % ss == 0 and sp <= ss
    n_sample = bs * ss
    n_tok = n_prompt + n_sample

    cos, sin = _rope_tables(ss)
    cg, sg = _dft_tables(FGROUP_DIM, FGROUP_DIM, FGROUP_DIM)
    dft = jnp.concatenate([cg, sg], axis=1).astype(BF16)
    seq_tabs = {s: _fft_tables(s) for s in {sp, ss}}

    row = lambda v: v.reshape(1, -1)
    gf = row(g_final)
    w_ff1_up, w_ff1_down, w_ff2_up, w_ff2_down, w_in, w_gate, w_pa, w_pf, w_o = (
        w.astype(BF16) for w in (w_ff1_up, w_ff1_down, w_ff2_up, w_ff2_down, w_in, w_gate, w_pa, w_pf, w_o))
    x = None
    for l in range(DEPTH):
        lam_init = 0.8 - 0.6 * math.exp(-0.3 * l)
        ff1 = (row(g_ff1[l]), w_ff1_up, w_ff1_down, gf)
        if l == 0:
            x = _ffn(x_prompt.reshape(n_prompt, D_MODEL), *ff1, layer=l, final_norm=False, out_rows=n_tok)
            x = _ffn(x_sample.reshape(n_sample, D_MODEL), *ff1, layer=l, final_norm=False, out_rows=n_tok,
                     out_row0=n_prompt, out_prev=x)
        else:
            x = _ffn(x, *ff1, layer=l, final_norm=False)
        q, kt, v, fab, gate = _pre(x, row(g_mix[l]), w_in, w_gate, cos, sin, dft, layer=l,
                                   n_prompt=n_prompt, s_prompt=sp, s_sample=ss)
        lamv = jnp.stack([lam_q1[l], lam_k1[l], lam_q2[l], lam_k2[l]])
        o = _attn(q, kt, v, lamv, row(g_sub[l]), None, row0=0, batch=bp, seq=sp, lam_init=lam_init)
        o = _attn(q, kt, v, lamv, row(g_sub[l]), o, row0=n_prompt, batch=bs, seq=ss, lam_init=lam_init)
        f = _seqfft(fab, *seq_tabs[sp], None, row0=0, batch=bp, seq=sp)
        f = _seqfft(fab, *seq_tabs[ss], f, row0=n_prompt, batch=bs, seq=ss)
        x = _post(x, o, f, gate, w_pa, w_pf, w_o, layer=l)
        ff2 = (row(g_ff2[l]), w_ff2_up, w_ff2_down, gf)
        if l < DEPTH - 1:
            x = _ffn(x, *ff2, layer=l, final_norm=False)
    last = DEPTH - 1
    y_prompt = _ffn(x, *ff2, layer=last, final_norm=True, rows=n_prompt)
    y_sample = _ffn(x, *ff2, layer=last, final_norm=True, rows=n_sample, in_row0=n_prompt)
    return y_prompt.reshape(bp, sp, D_MODEL), y_sample.reshape(bs, ss, D_MODEL)
```

```python
import functools
import math

import jax
import jax.numpy as jnp
from jax import lax
from jax.experimental import pallas as pl
from jax.experimental.pallas import tpu as pltpu

D_MODEL = 1024
DEPTH = 4
N_HEADS = 4
HEAD_DIM = 64
V_DIM = 2 * HEAD_DIM
QK_W = N_HEADS * 2 * HEAD_DIM
ATTN_W = N_HEADS * V_DIM
N_FGROUPS = 4
FGROUP_DIM = 128
FOURIER_W = N_FGROUPS * FGROUP_DIM
IN_W = 2 * QK_W + ATTN_W + FOURIER_W
D_FF = 2816
ROPE_THETA = 10000.0
EPS = 1e-6

LANES = 128
QKV_W = 2 * QK_W + ATTN_W
Q_SCALE = HEAD_DIM ** -0.5 * math.log2(math.e)
VMEM_LIMIT = 56 * 1024 * 1024
TOKEN_TILE = 1024
FFN_TILE = 1024
FUSED_TILE = 512
ATTN_TILE = 512 * 4096
FFT_RADIX = 16
F32_ROWS = 8
PERM_ROWS = 256
ROW_RUN = 256

F32 = jnp.float32
BF16 = jnp.bfloat16


def _rms(x, g):
    return x * lax.rsqrt(jnp.mean(x * x, axis=-1, keepdims=True) + EPS) * g


def _dot(a, b):
    return jnp.dot(a, b, preferred_element_type=F32)


def _params(sem):
    return pltpu.CompilerParams(dimension_semantics=sem, vmem_limit_bytes=VMEM_LIMIT)


def _const_spec(shape):
    return pl.BlockSpec(shape, lambda *_: (0,) * len(shape), pipeline_mode=pl.Buffered(1))


def _layer_spec(shape, layer, col_block=0):
    return pl.BlockSpec((None,) + shape, lambda *_: (layer, 0, col_block), pipeline_mode=pl.Buffered(1))


def _ffn_kernel(x_ref, g_ref, wg_ref, wu_ref, wd_ref, gf_ref, *rest, final_norm):
    o_ref = rest[-1]
    for part in range(x_ref.shape[0] // ROW_RUN):
        rows = slice(part * ROW_RUN, (part + 1) * ROW_RUN)
        x = x_ref[rows, :]
        h = _rms(x, g_ref[...]).astype(BF16)
        gate = _dot(h, wg_ref[...])
        up = _dot(h, wu_ref[...])
        act = (gate / (1.0 + jnp.exp(-gate)) * up).astype(BF16)
        y = x + 0.5 * _dot(act, wd_ref[...])
        if final_norm:
            y = _rms(y, gf_ref[...])
        o_ref[rows, :] = y


def _ffn(x, g, w_up, w_down, g_final, *, layer, final_norm, rows=None, in_row0=0, out_rows=None, out_row0=0,
         out_prev=None):
    tm = FFN_TILE
    rows = x.shape[0] if rows is None else rows
    out_rows = rows if out_rows is None else out_rows
    in0, out0 = in_row0 // tm, out_row0 // tm
    in_specs = [
        pl.BlockSpec((tm, D_MODEL), lambda i: (in0 + i, 0)),
        _const_spec((1, D_MODEL)),
        _layer_spec((D_MODEL, D_FF), layer, 0),
        _layer_spec((D_MODEL, D_FF), layer, 1),
        _layer_spec((D_FF, D_MODEL), layer),
        _const_spec((1, D_MODEL)),
    ]
    args = [x, g, w_up, w_up, w_down, g_final]
    aliases = {}
    if out_prev is not None:
        in_specs.append(pl.BlockSpec(memory_space=pl.ANY))
        args.append(out_prev)
        aliases = {len(args) - 1: 0}
    return pl.pallas_call(
        functools.partial(_ffn_kernel, final_norm=final_norm),
        grid=(rows // tm,),
        in_specs=in_specs,
        out_specs=pl.BlockSpec((tm, D_MODEL), lambda i: (out0 + i, 0)),
        out_shape=jax.ShapeDtypeStruct((out_rows, D_MODEL), F32),
        input_output_aliases=aliases,
        compiler_params=_params(("parallel",)),
        name="ffn",
    )(*args)


def _pre_kernel(x_ref, g_ref, win_ref, wgate_ref, cos_ref, sin_ref, dft_ref, perm_ref,
                q_ref, kt_ref, v_ref, fab_ref, gate_ref):
    lane = lax.broadcasted_iota(jnp.int32, (PERM_ROWS, LANES), 1)
    low_half = (lane % HEAD_DIM) < (HEAD_DIM // 2)
    dft = dft_ref[...]
    span = PERM_ROWS // FFT_RADIX
    for part in range(x_ref.shape[0] // PERM_ROWS):
        rows = slice(part * PERM_ROWS, (part + 1) * PERM_ROWS)
        h = _rms(x_ref[rows, :], g_ref[...]).astype(BF16)
        proj = _dot(h, win_ref[...])
        cos = cos_ref[rows, :]
        sin = sin_ref[rows, :]
        for j in range(2 * N_HEADS):
            t = proj[:, j * LANES:(j + 1) * LANES]
            rot = jnp.where(low_half,
                            pltpu.roll(t, LANES - HEAD_DIM // 2, axis=1),
                            pltpu.roll(t, HEAD_DIM // 2, axis=1))
            roped = t * cos + rot * sin
            if j < N_HEADS:
                q_ref[rows, j * LANES:(j + 1) * LANES] = (roped * Q_SCALE).astype(BF16)
            else:
                kt_ref[(j - N_HEADS) * LANES:(j - N_HEADS + 1) * LANES, rows] = roped.T.astype(BF16)
        v_ref[rows, :] = proj[:, 2 * QK_W:QKV_W].astype(BF16)
        up = _dot(perm_ref[...], proj[:, QKV_W:].astype(BF16)).astype(BF16)
        dst = slice(part * span, (part + 1) * span)
        for grp in range(N_FGROUPS):
            ab = _dot(up[:, grp * FGROUP_DIM:(grp + 1) * FGROUP_DIM], dft)
            for n2 in range(FFT_RADIX):
                src = slice(n2 * span, (n2 + 1) * span)
                col = n2 * 2 * FOURIER_W + grp * FGROUP_DIM
                fab_ref[dst, col:col + FGROUP_DIM] = ab[src, :FGROUP_DIM].astype(BF16)
                fab_ref[dst, col + FOURIER_W:col + FOURIER_W + FGROUP_DIM] = ab[src, FGROUP_DIM:].astype(BF16)
        z = _dot(h, wgate_ref[...])
        gate_ref[rows, :] = (1.0 / (1.0 + jnp.exp(-z))).astype(BF16)


def _pre(x, g, w_in, w_gate, cos, sin, dft, *, layer, n_prompt, s_prompt, s_sample):
    t = x.shape[0]
    tm = TOKEN_TILE
    n_prompt_tiles = n_prompt // tm
    idx = jnp.arange(PERM_ROWS)
    src = FFT_RADIX * (idx % (PERM_ROWS // FFT_RADIX)) + idx // (PERM_ROWS // FFT_RADIX)
    perm = (src[:, None] == idx[None, :]).astype(BF16)

    def pos_map(i):
        p = jnp.where(i < n_prompt_tiles, i % (s_prompt // tm), (i - n_prompt_tiles) % (s_sample // tm))
        return (p, 0)

    return pl.pallas_call(
        _pre_kernel,
        grid=(t // tm,),
        in_specs=[
            pl.BlockSpec((tm, D_MODEL), lambda i: (i, 0)),
            _const_spec((1, D_MODEL)),
            _layer_spec((D_MODEL, IN_W), layer),
            _layer_spec((D_MODEL, 2 * D_MODEL), layer),
            pl.BlockSpec((tm, LANES), pos_map),
            pl.BlockSpec((tm, LANES), pos_map),
            _const_spec((FGROUP_DIM, 2 * FGROUP_DIM)),
            _const_spec((PERM_ROWS, PERM_ROWS)),
        ],
        out_specs=[
            pl.BlockSpec((tm, QK_W), lambda i: (i, 0)),
            pl.BlockSpec((QK_W, tm), lambda i: (0, i)),
            pl.BlockSpec((tm, ATTN_W), lambda i: (i, 0)),
            pl.BlockSpec((tm // FFT_RADIX, FFT_RADIX * 2 * FOURIER_W), lambda i: (i, 0)),
            pl.BlockSpec((tm, 2 * D_MODEL), lambda i: (i, 0)),
        ],
        out_shape=[
            jax.ShapeDtypeStruct((t, QK_W), BF16),
            jax.ShapeDtypeStruct((QK_W, t), BF16),
            jax.ShapeDtypeStruct((t, ATTN_W), BF16),
            jax.ShapeDtypeStruct((t // FFT_RADIX, FFT_RADIX * 2 * FOURIER_W), BF16),
            jax.ShapeDtypeStruct((t, 2 * D_MODEL), BF16),
        ],
        compiler_params=_params(("parallel",)),
        name="mixer_in",
    )(x, g, w_in, w_gate, cos, sin, dft, perm)


def _attn_kernel(lamv_ref, q_ref, kt_ref, v_ref, gsub_ref, *rest, lam_init, n_items):
    o_ref = rest[-13]
    scr = rest[-12:]
    slots = tuple((scr[2 * k:2 * k + 2], scr[4 + 2 * k:6 + 2 * k], scr[8 + 2 * k:10 + 2 * k]) for k in range(2))
    t = pl.program_id(0)

    def scores(slot):
        s_refs, m_refs, _ = slots[slot]
        q = q_ref[...]
        kt = kt_ref[...]
        lane = lax.broadcasted_iota(jnp.int32, q.shape, 1)
        zero = jnp.zeros_like(q)
        for comp, qc in enumerate((jnp.where(lane < HEAD_DIM, q, zero), jnp.where(lane >= HEAD_DIM, q, zero))):
            s = _dot(qc, kt)
            s_refs[comp][...] = s
            m_refs[comp][...] = jnp.max(s, axis=-1, keepdims=True)

    def values(slot):
        s_refs, m_refs, r_refs = slots[slot]
        v = v_ref[...]
        v1 = jnp.concatenate([v, jnp.ones_like(v)], axis=1)
        for comp in range(2):
            r_refs[comp][...] = _dot(jnp.exp2(s_refs[comp][...] - m_refs[comp][...]).astype(BF16), v1)

    def output(slot):
        r0, r1 = (r[...] for r in slots[slot][2])
        lamv = lamv_ref[...]
        lam = (jnp.exp(jnp.sum(lamv[0:1] * lamv[1:2], axis=-1, keepdims=True))
               - jnp.exp(jnp.sum(lamv[2:3] * lamv[3:4], axis=-1, keepdims=True)) + lam_init)
        o = r0[:, :V_DIM] / r0[:, V_DIM:V_DIM + 1] - lam * (r1[:, :V_DIM] / r1[:, V_DIM:V_DIM + 1])
        o_ref[...] = (_rms(o, gsub_ref[...]) * (1.0 - lam_init)).astype(BF16)

    def stages(parity, do_scores, do_values, do_output):
        if do_output:
            output(parity)
        if do_scores:
            scores(parity)
        if do_values:
            values(1 - parity)

    @pl.when(t == 0)
    def _():
        stages(0, True, False, False)

    @pl.when(t == 1)
    def _():
        stages(1, True, True, False)

    for parity in range(2):
        @pl.when(jnp.logical_and(jnp.logical_and(t >= 2, t < n_items), t % 2 == parity))
        def _():
            stages(parity, True, True, True)

    @pl.when(t == n_items)
    def _():
        stages(n_items % 2, False, True, True)

    @pl.when(t == n_items + 1)
    def _():
        stages((n_items + 1) % 2, False, False, True)


def _attn(q, kt, v, lamv, gsub, o_prev, *, row0, batch, seq, lam_init):
    t = q.shape[0]
    tq = ATTN_TILE // seq
    b0 = row0 // seq
    q0 = row0 // tq
    nq = seq // tq
    n_items = batch * N_HEADS * nq

    def item(step):
        step = jnp.clip(step, 0, n_items - 1)
        return step // (N_HEADS * nq), (step // nq) % N_HEADS, step % nq

    def q_map(step):
        b, h, i = item(step)
        return (q0 + b * nq + i, h)

    def kt_map(step):
        b, h, _ = item(step)
        return (h, b0 + b)

    def v_map(step):
        b, h, _ = item(step - 1)
        return (b0 + b, h)

    def o_map(step):
        b, h, i = item(step - 2)
        return (q0 + b * nq + i, h)

    in_specs = [
        _const_spec((4, HEAD_DIM)),
        pl.BlockSpec((tq, LANES), q_map),
        pl.BlockSpec((LANES, seq), kt_map),
        pl.BlockSpec((seq, V_DIM), v_map),
        _const_spec((1, V_DIM)),
    ]
    args = [lamv, q, kt, v, gsub]
    aliases = {}
    if o_prev is not None:
        in_specs.append(pl.BlockSpec(memory_space=pl.ANY))
        args.append(o_prev)
        aliases = {len(args) - 1: 0}
    return pl.pallas_call(
        functools.partial(_attn_kernel, lam_init=lam_init, n_items=n_items),
        grid=(n_items + 2,),
        in_specs=in_specs,
        out_specs=pl.BlockSpec((tq, V_DIM), o_map),
        out_shape=jax.ShapeDtypeStruct((t, ATTN_W), BF16),
        scratch_shapes=([pltpu.VMEM((tq, seq), F32)] * 4 + [pltpu.VMEM((tq, 1), F32)] * 4
                        + [pltpu.VMEM((tq, 2 * V_DIM), F32)] * 4),
        input_output_aliases=aliases,
        compiler_params=_params(("arbitrary",)),
        name=f"diff_attn_s{seq}",
    )(*args)


def _cmul_const(x, w):
    re, im = x
    wr, wi = round(w.real, 12), round(w.imag, 12)
    if (wr, wi) == (1.0, 0.0):
        return re, im
    if (wr, wi) == (-1.0, 0.0):
        return -re, -im
    if (wr, wi) == (0.0, -1.0):
        return im, -re
    if (wr, wi) == (0.0, 1.0):
        return -im, re
    return re * wr - im * wi, re * wi + im * wr


def _fft(xs):
    n = len(xs)
    if n == 1:
        return xs
    even, odd = _fft(xs[0::2]), _fft(xs[1::2])
    out = [None] * n
    for k in range(n // 2):
        tr, ti = _cmul_const(odd[k], complex(math.cos(2 * math.pi * k / n), -math.sin(2 * math.pi * k / n)))
        er, ei = even[k]
        out[k] = (er + tr, ei + ti)
        out[k + n // 2] = (er - tr, ei - ti)
    return out


def _seqfft_kernel(m1_ref, twc_ref, tws_ref, fab_ref, *rest, scale):
    o_ref, wre, wim = rest[-3:]
    n = m1_ref.shape[1]
    for n2 in range(FFT_RADIX):
        ab = fab_ref[:, n2 * 2 * FOURIER_W:(n2 + 1) * 2 * FOURIER_W]
        p = _dot(m1_ref[...], ab)
        t_re = p[:n, :FOURIER_W] - p[n:, FOURIER_W:]
        t_im = -(p[:n, FOURIER_W:] + p[n:, :FOURIER_W])
        c = jnp.concatenate([twc_ref[n2]] * (FOURIER_W // LANES), axis=1)
        s = jnp.concatenate([tws_ref[n2]] * (FOURIER_W // LANES), axis=1)
        wre[n2] = t_re * c + t_im * s
        wim[n2] = t_im * c - t_re * s
    for g in range(n // F32_ROWS):
        rows = slice(g * F32_ROWS, (g + 1) * F32_ROWS)
        for lb in range(FOURIER_W // LANES):
            cols = slice(lb * LANES, (lb + 1) * LANES)
            ys = _fft([(wre[n2, rows, cols], wim[n2, rows, cols]) for n2 in range(FFT_RADIX)])
            for k2 in range(FFT_RADIX):
                o_ref[k2 * n + g * F32_ROWS:k2 * n + (g + 1) * F32_ROWS, cols] = (ys[k2][0] * scale).astype(BF16)


def _fft_tables(seq):
    n = seq // FFT_RADIX
    cr, sr = _dft_tables(n, n, n)
    twc, tws = _dft_tables(FFT_RADIX, n, seq)
    lanes = lambda a: jnp.broadcast_to(a[:, :, None], (FFT_RADIX, n, LANES))
    return jnp.concatenate([cr, sr], axis=0).astype(BF16), lanes(twc), lanes(tws)


def _seqfft(fab, m1, twc, tws, f_prev, *, row0, batch, seq):
    t = fab.shape[0] * FFT_RADIX
    n = seq // FFT_RADIX
    b0 = row0 // seq
    in_specs = [
        _const_spec((2 * n, n)),
        _const_spec((FFT_RADIX, n, LANES)),
        _const_spec((FFT_RADIX, n, LANES)),
        pl.BlockSpec((n, FFT_RADIX * 2 * FOURIER_W), lambda b: (b0 + b, 0)),
    ]
    args = [m1, twc, tws, fab]
    aliases = {}
    if f_prev is not None:
        in_specs.append(pl.BlockSpec(memory_space=pl.ANY))
        args.append(f_prev)
        aliases = {len(args) - 1: 0}
    return pl.pallas_call(
        functools.partial(_seqfft_kernel, scale=1.0 / math.sqrt(seq * FGROUP_DIM)),
        grid=(batch,),
        in_specs=in_specs,
        out_specs=pl.BlockSpec((seq, FOURIER_W), lambda b: (b0 + b, 0)),
        out_shape=jax.ShapeDtypeStruct((t, FOURIER_W), BF16),
        scratch_shapes=[pltpu.VMEM((FFT_RADIX, n, FOURIER_W), F32)] * 2,
        input_output_aliases=aliases,
        compiler_params=_params(("parallel",)),
        name=f"seq_fft_s{seq}",
    )(*args)


def _post_kernel(x_ref, o_ref, f_ref, gate_ref, wpa_ref, wpf_ref, wo_ref, out_ref):
    br_a = _dot(o_ref[...], wpa_ref[...])
    br_f = _dot(f_ref[...], wpf_ref[...])
    merged = (gate_ref[:, :D_MODEL].astype(F32) * br_a + gate_ref[:, D_MODEL:].astype(F32) * br_f)
    out_ref[...] = x_ref[...] + _dot(merged.astype(BF16), wo_ref[...])


def _post(x, o, f, gate, w_pa, w_pf, w_o, *, layer):
    t = x.shape[0]
    tm = TOKEN_TILE
    return pl.pallas_call(
        _post_kernel,
        grid=(t // tm,),
        in_specs=[
            pl.BlockSpec((tm, D_MODEL), lambda i: (i, 0)),
            pl.BlockSpec((tm, ATTN_W), lambda i: (i, 0)),
            pl.BlockSpec((tm, FOURIER_W), lambda i: (i, 0)),
            pl.BlockSpec((tm, 2 * D_MODEL), lambda i: (i, 0)),
            _layer_spec((ATTN_W, D_MODEL), layer),
            _layer_spec((FOURIER_W, D_MODEL), layer),
            _layer_spec((D_MODEL, D_MODEL), layer),
        ],
        out_specs=pl.BlockSpec((tm, D_MODEL), lambda i: (i, 0)),
        out_shape=jax.ShapeDtypeStruct((t, D_MODEL), F32),
        compiler_params=_params(("parallel",)),
        name="mixer_out",
    )(x, o, f, gate, w_pa, w_pf, w_o)


def _post_ffn_kernel(x_ref, o_ref, f_ref, gate_ref, wpa_ref, wpf_ref, wo_ref, g_ref, wg_ref, wu_ref, wd_ref,
                     out_ref):
    for part in range(x_ref.shape[0] // ROW_RUN):
        rows = slice(part * ROW_RUN, (part + 1) * ROW_RUN)
        br_a = _dot(o_ref[rows, :], wpa_ref[...])
        br_f = _dot(f_ref[rows, :], wpf_ref[...])
        merged = (gate_ref[rows, :D_MODEL].astype(F32) * br_a + gate_ref[rows, D_MODEL:].astype(F32) * br_f)
        x = x_ref[rows, :] + _dot(merged.astype(BF16), wo_ref[...])
        h = _rms(x, g_ref[...]).astype(BF16)
        gate = _dot(h, wg_ref[...])
        up = _dot(h, wu_ref[...])
        act = (gate / (1.0 + jnp.exp(-gate)) * up).astype(BF16)
        out_ref[rows, :] = x + 0.5 * _dot(act, wd_ref[...])


def _post_ffn(x, o, f, gate, w_pa, w_pf, w_o, g, w_up, w_down, *, layer):
    t = x.shape[0]
    tm = FUSED_TILE
    tile = lambda w: pl.BlockSpec((tm, w), lambda i: (i, 0))
    return pl.pallas_call(
        _post_ffn_kernel,
        grid=(t // tm,),
        in_specs=[
            tile(D_MODEL), tile(ATTN_W), tile(FOURIER_W), tile(2 * D_MODEL),
            _layer_spec((ATTN_W, D_MODEL), layer),
            _layer_spec((FOURIER_W, D_MODEL), layer),
            _layer_spec((D_MODEL, D_MODEL), layer),
            _const_spec((1, D_MODEL)),
            _layer_spec((D_MODEL, D_FF), layer, 0),
            _layer_spec((D_MODEL, D_FF), layer, 1),
            _layer_spec((D_FF, D_MODEL), layer),
        ],
        out_specs=tile(D_MODEL),
        out_shape=jax.ShapeDtypeStruct((t, D_MODEL), F32),
        compiler_params=_params(("parallel",)),
        name="mixer_out_ffn",
    )(x, o, f, gate, w_pa, w_pf, w_o, g, w_up, w_up, w_down)


def _rope_tables(seq):
    inv = 1.0 / (ROPE_THETA ** (jnp.arange(0, HEAD_DIM, 2, dtype=F32) / HEAD_DIM))
    ang = jnp.arange(seq, dtype=F32)[:, None] * inv[None, :]
    ang = jnp.concatenate([ang, ang, ang, ang], axis=-1)
    lane = jnp.arange(LANES)
    sign = jnp.where((lane % HEAD_DIM) < HEAD_DIM // 2, -1.0, 1.0).astype(F32)
    return jnp.cos(ang), jnp.sin(ang) * sign[None, :]


def _dft_tables(rows, cols, n):
    j = jnp.arange(rows, dtype=jnp.int32)
    k = jnp.arange(cols, dtype=jnp.int32)
    ang = ((j[:, None] * k[None, :]) % n).astype(F32) * (2.0 * math.pi / n)
    return jnp.cos(ang), jnp.sin(ang)


def kernel(x_prompt, x_sample, g_ff1, w_ff1_up, w_ff1_down, g_mix, w_in, lam_q1, lam_k1, lam_q2, lam_k2,
           g_sub, w_pa, w_pf, w_gate, w_o, g_ff2, w_ff2_up, w_ff2_down, g_final):
    bp, sp, _ = x_prompt.shape
    bs, ss, _ = x_sample.shape
    n_prompt = bp * sp
    assert n_prompt % ss == 0 and sp <= ss
    n_sample = bs * ss
    n_tok = n_prompt + n_sample

    cos, sin = _rope_tables(ss)
    cg, sg = _dft_tables(FGROUP_DIM, FGROUP_DIM, FGROUP_DIM)
    dft = jnp.concatenate([cg, sg], axis=1).astype(BF16)
    seq_tabs = {s: _fft_tables(s) for s in {sp, ss}}

    row = lambda v: v.reshape(1, -1)
    gf = row(g_final)
    w_ff1_up, w_ff1_down, w_ff2_up, w_ff2_down, w_in, w_gate, w_pa, w_pf, w_o = (
        w.astype(BF16) for w in (w_ff1_up, w_ff1_down, w_ff2_up, w_ff2_down, w_in, w_gate, w_pa, w_pf, w_o))
    x = None
    for l in range(DEPTH):
        lam_init = 0.8 - 0.6 * math.exp(-0.3 * l)
        ff1 = (row(g_ff1[l]), w_ff1_up, w_ff1_down, gf)
        if l == 0:
            x = _ffn(x_prompt.reshape(n_prompt, D_MODEL), *ff1, layer=l, final_norm=False, out_rows=n_tok)
            x = _ffn(x_sample.reshape(n_sample, D_MODEL), *ff1, layer=l, final_norm=False, out_rows=n_tok,
                     out_row0=n_prompt, out_prev=x)
        else:
            x = _ffn(x, *ff1, layer=l, final_norm=False)
        q, kt, v, fab, gate = _pre(x, row(g_mix[l]), w_in, w_gate, cos, sin, dft, layer=l,
                                   n_prompt=n_prompt, s_prompt=sp, s_sample=ss)
        lamv = jnp.stack([lam_q1[l], lam_k1[l], lam_q2[l], lam_k2[l]])
        o = _attn(q, kt, v, lamv, row(g_sub[l]), None, row0=0, batch=bp, seq=sp, lam_init=lam_init)
        o = _attn(q, kt, v, lamv, row(g_sub[l]), o, row0=n_prompt, batch=bs, seq=ss, lam_init=lam_init)
        f = _seqfft(fab, *seq_tabs[sp], None, row0=0, batch=bp, seq=sp)
        f = _seqfft(fab, *seq_tabs[ss], f, row0=n_prompt, batch=bs, seq=ss)
        ff2 = (row(g_ff2[l]), w_ff2_up, w_ff2_down, gf)
        if l < DEPTH - 1:
            x = _post_ffn(x, o, f, gate, w_pa, w_pf, w_o, *ff2[:3], layer=l)
        else:
            x = _post(x, o, f, gate, w_pa, w_pf, w_o, layer=l)
    last = DEPTH - 1
    y_prompt = _ffn(x, *ff2, layer=last, final_norm=True, rows=n_prompt)
    y_sample = _ffn(x, *ff2, layer=last, final_norm=True, rows=n_sample, in_row0=n_prompt)
    return y_prompt.reshape(bp, sp, D_MODEL), y_sample.reshape(bs, ss, D_MODEL)
```
